```python
import math
import jax, jax.numpy as jnp
from jax import lax
import numpy as np

D_MODEL = 2048
BATCH = 4
SEQ = 4096
DEPTH = 1

CHUNK = 64
N_META = 16
Q_BLOCK = 128
ROPE_THETA = 10000.0
LN_EPS = 1e-5
ATT_HALF_DIM = 64
N_ATT_HEADS = D_MODEL // (4 * ATT_HALF_DIM)
ATT_QK_WIDTH = N_ATT_HEADS * 2 * ATT_HALF_DIM
ATT_V_WIDTH = N_ATT_HEADS * 2 * ATT_HALF_DIM
CONV_DIM = D_MODEL // 2
CONV_WIDTH = 3
IN_COLS = 2 * ATT_QK_WIDTH + ATT_V_WIDTH + 3 * CONV_DIM + 2 * D_MODEL
N_EXPERTS = 64
TOP_K = 8
N_GROUPS = 8
TOPK_GROUPS = 4
D_EXPERT = D_MODEL // 4
ROUTED_SCALE = 2.5
EXPERT_BLOCK = 128
DEEPNORM_ALPHA = (2 * DEPTH) ** 0.25
DEEPNORM_BETA = (8 * DEPTH) ** -0.25

kernel_name = "chunk_causal_diffattn_shortconv_moe_hybrid"


def _layer_norm(x, g, b):
    xf = x.astype(jnp.float32)
    mu = jnp.mean(xf, axis=-1, keepdims=True)
    var = jnp.mean(jnp.square(xf - mu), axis=-1, keepdims=True)
    return ((xf - mu) * lax.rsqrt(var + LN_EPS) * g.astype(jnp.float32) + b.astype(jnp.float32)).astype(x.dtype)


def _rms_norm(x, g):
    xf = x.astype(jnp.float32)
    return (xf * lax.rsqrt(jnp.mean(xf * xf, axis=-1, keepdims=True) + LN_EPS) * g.astype(jnp.float32)).astype(x.dtype)


def _rope(x, pos):
    half = x.shape[-1] // 2
    inv_freq = ROPE_THETA ** (-jnp.arange(half, dtype=jnp.float32) / half)
    ang = pos.astype(jnp.float32)[:, None] * inv_freq[None, :]
    cos = jnp.cos(ang)[None, :, None, None, :]
    sin = jnp.sin(ang)[None, :, None, None, :]
    x1 = x[..., :half].astype(jnp.float32)
    x2 = x[..., half:].astype(jnp.float32)
    return jnp.concatenate([x1 * cos - x2 * sin, x2 * cos + x1 * sin], axis=-1).astype(x.dtype)


def _diff_attn_block(q, k, v, q_cid, k_cid, lam):
    s = jnp.einsum('bqhmd,bkhmd->bhmqk', q, k, preferred_element_type=jnp.float32) * (ATT_HALF_DIM ** -0.5)
    visible = k_cid[None, :] <= q_cid[:, None]
    p = jax.nn.softmax(jnp.where(visible, s, -jnp.inf), axis=-1)
    a = p[:, :, 0] - lam * p[:, :, 1]
    return jnp.einsum('bhqk,bkhe->bqhe', a.astype(v.dtype), v)


def _mixer(h, q0, lam_init, w_in, lambda_q1, lambda_k1, lambda_q2, lambda_k2, subln_g,
           w_conv, w_proj_attn, w_proj_conv, w_out):
    B, L, _ = h.shape
    u = h @ w_in
    splits = [int(c) for c in np.cumsum([ATT_QK_WIDTH, ATT_QK_WIDTH, ATT_V_WIDTH,
                                          CONV_DIM, CONV_DIM, CONV_DIM, D_MODEL])]
    q, k, v, cx, cc, cb, ga, gc = jnp.split(u, splits, axis=-1)

    pos = jnp.arange(L, dtype=jnp.int32)
    cid = jnp.where(pos < N_META, -1, (pos - N_META) // CHUNK)

    q = _rope(q.reshape(B, L, N_ATT_HEADS, 2, ATT_HALF_DIM), pos)
    k = _rope(k.reshape(B, L, N_ATT_HEADS, 2, ATT_HALF_DIM), pos)
    v = v.reshape(B, L, N_ATT_HEADS, 2 * ATT_HALF_DIM)
    f32 = jnp.float32
    lam = (jnp.exp(jnp.sum(lambda_q1.astype(f32) * lambda_k1.astype(f32)))
           - jnp.exp(jnp.sum(lambda_q2.astype(f32) * lambda_k2.astype(f32))) + lam_init)
    blocks = [(0, N_META)] if q0 == 0 else []
    blocks = blocks + [(s, s + Q_BLOCK) for s in range(N_META, L, Q_BLOCK)]
    outs = [_diff_attn_block(q[:, s:e], k[:, :e], v[:, :e], cid[s:e], cid[:e], lam) for s, e in blocks]
    o = jnp.concatenate(outs, axis=1)
    o = (_rms_norm(o, subln_g) * (1.0 - lam_init)).reshape(B, L - q0, ATT_V_WIDTH)

    z = cc * cx
    y = lax.conv_general_dilated(z, w_conv[:, None, :].astype(z.dtype), window_strides=(1,),
                                 padding=[(CONV_WIDTH - 1, 0)],
                                 dimension_numbers=('NWC', 'WIO', 'NWC'),
                                 feature_group_count=CONV_DIM)
    y = cb[:, q0:] * y[:, q0:]

    merged = (jax.nn.sigmoid(ga[:, q0:]) * (o @ w_proj_attn)
              + jax.nn.sigmoid(gc[:, q0:]) * (y @ w_proj_conv))
    return merged @ w_out


def _routed_experts(xf, eidx, ew, w_exp_gate, w_exp_up, w_exp_down):
    N, D = xf.shape
    A = N * TOP_K
    flat_e = eidx.reshape(A)
    flat_tok = jnp.arange(A, dtype=jnp.int32) // TOP_K
    flat_w = ew.reshape(A)
    order = jnp.argsort(flat_e)
    e_sorted = flat_e[order]
    counts = jnp.bincount(flat_e, length=N_EXPERTS)
    start = jnp.cumsum(counts) - counts
    padded = (counts + EXPERT_BLOCK - 1) // EXPERT_BLOCK * EXPERT_BLOCK
    pad_end = jnp.cumsum(padded)
    pad_start = pad_end - padded
    dest = pad_start[e_sorted] + jnp.arange(A, dtype=jnp.int32) - start[e_sorted]
    n_blocks = -(-A // EXPERT_BLOCK) + N_EXPERTS
    P = n_blocks * EXPERT_BLOCK
    row_tok = jnp.full((P,), N, jnp.int32).at[dest].set(flat_tok[order])
    row_w = jnp.zeros((P,), xf.dtype).at[dest].set(flat_w[order].astype(xf.dtype))
    block_e = jnp.minimum(jnp.searchsorted(pad_end, jnp.arange(n_blocks, dtype=jnp.int32) * EXPERT_BLOCK,
                                           side='right'), N_EXPERTS - 1)
    x_pad = jnp.concatenate([xf, jnp.zeros((1, D), xf.dtype)], axis=0)

    def block_fn(args):
        e, toks, wts = args
        xb = x_pad[toks]
        hb = jax.nn.silu(xb @ w_exp_gate[e]) * (xb @ w_exp_up[e])
        return (hb @ w_exp_down[e]) * wts[:, None]

    yb = lax.map(block_fn, (block_e, row_tok.reshape(n_blocks, EXPERT_BLOCK),
                            row_w.reshape(n_blocks, EXPERT_BLOCK)))
    return jax.ops.segment_sum(yb.reshape(P, D), row_tok, num_segments=N + 1)[:N]


def _moe(h, w_router, router_bias, w_exp_gate, w_exp_up, w_exp_down, w_sh_gate, w_sh_up, w_sh_down):
    B, T, D = h.shape
    xf = h.reshape(B * T, D)
    N = B * T
    logits = xf.astype(jnp.float32) @ w_router.astype(jnp.float32)
    s = jax.nn.sigmoid(logits)
    sel = s + router_bias.astype(jnp.float32)
    per_group = N_EXPERTS // N_GROUPS
    gscore = lax.top_k(sel.reshape(N, N_GROUPS, per_group), 2)[0].sum(-1)
    _, gidx = lax.top_k(gscore, TOPK_GROUPS)
    gmask = jax.nn.one_hot(gidx, N_GROUPS, dtype=jnp.float32).sum(-2) > 0
    sel = jnp.where(jnp.repeat(gmask, per_group, axis=-1), sel, -jnp.inf)
    _, eidx = lax.top_k(sel, TOP_K)
    ew = jnp.take_along_axis(s, eidx, axis=-1)
    ew = ew / jnp.sum(ew, axis=-1, keepdims=True) * ROUTED_SCALE
    shared = (jax.nn.silu(xf @ w_sh_gate) * (xf @ w_sh_up)) @ w_sh_down
    routed = _routed_experts(xf, eidx, ew, w_exp_gate, w_exp_up, w_exp_down)
    return (shared + routed).reshape(B, T, D)


def setup_inputs(seed: int = 0) -> dict:
    key = jax.random.key(seed)
    ks = jax.random.split(key, 32)

    def nrm(k, shape, scale):
        return jax.random.normal(k, shape, jnp.float32) * scale

    D, L_ = D_MODEL, DEPTH
    beta = DEEPNORM_BETA
    col_scale = jnp.concatenate([
        jnp.ones((2 * ATT_QK_WIDTH,), jnp.float32),
        jnp.full((ATT_V_WIDTH + CONV_DIM,), beta, jnp.float32),
        jnp.ones((2 * CONV_DIM + 2 * D_MODEL,), jnp.float32)])
    return {
        "x": nrm(ks[0], (BATCH, SEQ, D), 1.0),
        "meta_tokens": nrm(ks[1], (N_META, D), 1.0),
        "ln0_g": 1.0 + nrm(ks[2], (D,), 0.02),
        "ln0_b": nrm(ks[3], (D,), 0.02),
        "w_in": nrm(ks[4], (L_, D, IN_COLS), D ** -0.5) * col_scale,
        "lambda_q1": nrm(ks[5], (L_, ATT_HALF_DIM), 0.1),
        "lambda_k1": nrm(ks[6], (L_, ATT_HALF_DIM), 0.1),
        "lambda_q2": nrm(ks[7], (L_, ATT_HALF_DIM), 0.1),
        "lambda_k2": nrm(ks[8], (L_, ATT_HALF_DIM), 0.1),
        "subln_g": 1.0 + nrm(ks[9], (L_, 2 * ATT_HALF_DIM), 0.02),
        "w_conv": nrm(ks[10], (L_, CONV_WIDTH, CONV_DIM), CONV_WIDTH ** -0.5),
        "w_proj_attn": nrm(ks[11], (L_, ATT_V_WIDTH, D), ATT_V_WIDTH ** -0.5 * beta),
        "w_proj_conv": nrm(ks[12], (L_, CONV_DIM, D), CONV_DIM ** -0.5 * beta),
        "w_out": nrm(ks[13], (L_, D, D), D ** -0.5 * beta),
        "ln1_g": 1.0 + nrm(ks[14], (L_, D), 0.02),
        "ln1_b": nrm(ks[15], (L_, D), 0.02),
        "w_router": nrm(ks[16], (L_, D, N_EXPERTS), D ** -0.5),
        "router_bias": nrm(ks[17], (L_, N_EXPERTS), 0.01),
        "w_exp_gate": nrm(ks[18], (L_, N_EXPERTS, D, D_EXPERT), D ** -0.5 * beta),
        "w_exp_up": nrm(ks[19], (L_, N_EXPERTS, D, D_EXPERT), D ** -0.5 * beta),
        "w_exp_down": nrm(ks[20], (L_, N_EXPERTS, D_EXPERT, D), D_EXPERT ** -0.5 * beta),
        "w_sh_gate": nrm(ks[21], (L_, D, D_EXPERT), D ** -0.5 * beta),
        "w_sh_up": nrm(ks[22], (L_, D, D_EXPERT), D ** -0.5 * beta),
        "w_sh_down": nrm(ks[23], (L_, D_EXPERT, D), D_EXPERT ** -0.5 * beta),
        "ln2_g": 1.0 + nrm(ks[24], (L_, D), 0.02),
        "ln2_b": nrm(ks[25], (L_, D), 0.02),
    }


def reference(x, meta_tokens, ln0_g, ln0_b, w_in, lambda_q1, lambda_k1, lambda_q2, lambda_k2,
              subln_g, w_conv, w_proj_attn, w_proj_conv, w_out, ln1_g, ln1_b, w_router,
              router_bias, w_exp_gate, w_exp_up, w_exp_down, w_sh_gate, w_sh_up, w_sh_down,
              ln2_g, ln2_b):
    B = x.shape[0]
    meta = jnp.broadcast_to(meta_tokens[None].astype(x.dtype), (B, N_META, D_MODEL))
    h = _layer_norm(jnp.concatenate([meta, x], axis=1), ln0_g, ln0_b)
    for l in range(DEPTH):
        q0 = N_META if l == DEPTH - 1 else 0
        lam_init = 0.8 - 0.6 * math.exp(-0.3 * l)
        m = _mixer(h, q0, lam_init, w_in[l], lambda_q1[l], lambda_k1[l], lambda_q2[l], lambda_k2[l],
                   subln_g[l], w_conv[l], w_proj_attn[l], w_proj_conv[l], w_out[l])
        h1 = _layer_norm(DEEPNORM_ALPHA * h[:, q0:] + m, ln1_g[l], ln1_b[l])
        f = _moe(h1, w_router[l], router_bias[l], w_exp_gate[l], w_exp_up[l], w_exp_down[l],
                 w_sh_gate[l], w_sh_up[l], w_sh_down[l])
        h = _layer_norm(DEEPNORM_ALPHA * h1 + f, ln2_g[l], ln2_b[l])
    return h
```

```python
import functools

import jax
import jax.numpy as jnp
import numpy as np
from jax import lax
from jax.experimental import pallas as pl
from jax.experimental.pallas import tpu as pltpu

CHUNK = 64
ROPE_THETA = 10000.0
LN_EPS = 1e-5
TOP_K = 8
N_GROUPS = 8
TOPK_GROUPS = 4
ROUTED_SCALE = 2.5
LAM_INIT = 0.2

LANES = 128
BF16_SUBLANES = 16
VMEM_LIMIT_BYTES = 56 * 1024 * 1024

F32 = jnp.float32
BF16 = jnp.bfloat16


def _params(*sem):
    return pltpu.CompilerParams(dimension_semantics=sem, vmem_limit_bytes=VMEM_LIMIT_BYTES)


def _layer_norm(x, g, b):
    mu = jnp.mean(x, axis=-1, keepdims=True)
    xc = x - mu
    var = jnp.mean(xc * xc, axis=-1, keepdims=True)
    return xc * lax.rsqrt(var + LN_EPS) * g + b


def _dot(a, b):
    return jnp.dot(a, b, preferred_element_type=F32)


def _dot_nt(a, b, **kw):
    return lax.dot_general(a, b, (((1,), (1,)), ((), ())), preferred_element_type=F32, **kw)


def _sigmoid(x):
    return 1.0 / (1.0 + jnp.exp(-x))


def _inproj_kernel(x_ref, g_ref, b_ref, w_ref, cos_ref, sin_ref, u_ref, h_ref, hb_ref,
                   *, q_tiles, qk_scale, half):
    j = pl.program_id(1)

    @pl.when(j == 0)
    def _():
        h = _layer_norm(x_ref[...], g_ref[...], b_ref[...])
        h_ref[...] = h
        hb_ref[...] = h.astype(BF16)

    acc = _dot(hb_ref[...], w_ref[...])
    tn = acc.shape[1]

    @pl.when(j < 2 * q_tiles)
    def _():
        cos = cos_ref[...]
        sin = sin_ref[...]
        lane = lax.broadcasted_iota(jnp.int32, cos.shape, 1)
        first = (lane % (2 * half)) < half
        scale = jnp.where(j < q_tiles, qk_scale, 1.0).astype(F32)
        for c in range(tn // LANES):
            a = acc[:, c * LANES:(c + 1) * LANES]
            partner = jnp.where(first, pltpu.roll(a, LANES - half, axis=1), pltpu.roll(a, half, axis=1))
            u_ref[:, c * LANES:(c + 1) * LANES] = ((a * cos + partner * sin) * scale).astype(BF16)

    @pl.when(j >= 2 * q_tiles)
    def _():
        u_ref[...] = acc.astype(BF16)


def _in_proj(x2d, g, b, w_bf, cos, sin, *, tm, tn, qk_width, dh):
    rows, d = x2d.shape
    cols = w_bf.shape[1]
    pos_blocks = cos.shape[0] // tm
    kern = functools.partial(_inproj_kernel, q_tiles=qk_width // tn, qk_scale=float(dh) ** -0.5, half=dh // 2)
    return pl.pallas_call(
        kern,
        grid=(rows // tm, cols // tn),
        in_specs=[
            pl.BlockSpec((tm, d), lambda i, j: (i, 0)),
            pl.BlockSpec((1, d), lambda i, j: (0, 0)),
            pl.BlockSpec((1, d), lambda i, j: (0, 0)),
            pl.BlockSpec((d, tn), lambda i, j: (0, j)),
            pl.BlockSpec((tm, LANES), lambda i, j: (i % pos_blocks, 0)),
            pl.BlockSpec((tm, LANES), lambda i, j: (i % pos_blocks, 0)),
        ],
        out_specs=[
            pl.BlockSpec((tm, tn), lambda i, j: (i, j)),
            pl.BlockSpec((tm, d), lambda i, j: (i, 0)),
        ],
        out_shape=[
            jax.ShapeDtypeStruct((rows, cols), BF16),
            jax.ShapeDtypeStruct((rows, d), F32),
        ],
        scratch_shapes=[pltpu.VMEM((tm, d), BF16)],
        compiler_params=_params("arbitrary", "arbitrary"),
        name="in_proj",
    )(x2d, g, b, w_bf, cos, sin)


def _attn_kernel(q_ref, k_ref, v_ref, km_ref, vm_ref, lamv_ref, g_ref, o_ref, *, tq, dh):
    qi = pl.program_id(2)
    q = q_ref[...]
    lane = lax.broadcasted_iota(jnp.int32, q.shape, 1)
    zero = jnp.zeros_like(q)
    qq = jnp.concatenate([jnp.where(lane < dh, q, zero), jnp.where(lane >= dh, q, zero)], axis=0)

    s = _dot_nt(qq, km_ref[...])
    m = jnp.max(s, axis=1, keepdims=True)
    p = jnp.exp(s - m)
    l = jnp.sum(p, axis=1, keepdims=True)
    acc = _dot(p.astype(BF16), vm_ref[...])

    def update(carry, s, vs):
        m, l, acc = carry
        m_new = jnp.maximum(m, jnp.max(s, axis=1, keepdims=True))
        a = jnp.exp(m - m_new)
        p = jnp.exp(s - m_new)
        l = a * l + jnp.sum(p, axis=1, keepdims=True)
        acc = a * acc + _dot(p.astype(BF16), vs)
        return m_new, l, acc

    def full_block(kb, carry):
        start = pl.multiple_of(kb * tq, tq)
        ks = k_ref[pl.ds(start, tq), :]
        vs = v_ref[pl.ds(start, tq), :]
        return update(carry, _dot_nt(qq, ks), vs)

    carry = lax.fori_loop(0, qi, full_block, (m, l, acc))

    start = pl.multiple_of(qi * tq, tq)
    ks = k_ref[pl.ds(start, tq), :]
    vs = v_ref[pl.ds(start, tq), :]
    s = _dot_nt(qq, ks)
    row = lax.broadcasted_iota(jnp.int32, s.shape, 0)
    row = jnp.where(row >= tq, row - tq, row)
    col = lax.broadcasted_iota(jnp.int32, s.shape, 1)
    s = jnp.where((col // CHUNK) <= (row // CHUNK), s, -jnp.inf)
    m, l, acc = update(carry, s, vs)

    lamv = lamv_ref[...]
    lam = (jnp.exp(jnp.sum(lamv[0:1] * lamv[1:2], axis=1, keepdims=True))
           - jnp.exp(jnp.sum(lamv[2:3] * lamv[3:4], axis=1, keepdims=True)) + LAM_INIT)
    o_all = acc / l
    o = o_all[:tq] - lam * o_all[tq:]
    ms = jnp.mean(o * o, axis=-1, keepdims=True)
    o = o * lax.rsqrt(ms + LN_EPS) * g_ref[...] * (1.0 - LAM_INIT)
    o_ref[...] = o.astype(BF16)


def _attention(u, u_meta, lamv, subln_g, *, batch, seq, heads, dh, tq):
    vd = 2 * dh
    nq = seq // tq
    kern = functools.partial(_attn_kernel, tq=tq, dh=dh)
    n_meta = u_meta.shape[0]
    return pl.pallas_call(
        kern,
        grid=(batch, heads, nq),
        in_specs=[
            pl.BlockSpec((tq, vd), lambda b, h, i: (b * nq + i, h)),
            pl.BlockSpec((seq, vd), lambda b, h, i: (b, heads + h)),
            pl.BlockSpec((seq, vd), lambda b, h, i: (b, 2 * heads + h)),
            pl.BlockSpec((n_meta, vd), lambda b, h, i: (0, heads + h)),
            pl.BlockSpec((n_meta, vd), lambda b, h, i: (0, 2 * heads + h)),
            pl.BlockSpec((4, dh), lambda b, h, i: (0, 0)),
            pl.BlockSpec((1, vd), lambda b, h, i: (0, 0)),
        ],
        out_specs=pl.BlockSpec((tq, vd), lambda b, h, i: (b * nq + i, h)),
        out_shape=jax.ShapeDtypeStruct((batch * seq, heads * vd), BF16),
        compiler_params=_params("arbitrary", "arbitrary", "arbitrary"),
        name="attention",
    )(u, u, u, u_meta, u_meta, lamv, subln_g)


def _merge_kernel(o_ref, cx_ref, cc_ref, cb_ref, hx_ref, hc_ref, mx_ref, mc_ref, wconv_ref,
                  ga_ref, gc_ref, wa_ref, wc_ref, out_ref, y_ref, *, tiles_per_seq):
    i = pl.program_id(0)
    j = pl.program_id(1)

    @pl.when(j == 0)
    def _():
        z = cc_ref[...].astype(F32) * cx_ref[...].astype(F32)
        cb = cb_ref[...].astype(F32)
        w = wconv_ref[...]
        w0, w1, w2 = w[0:1], w[1:2], w[2:3]
        y_ref[...] = (cb * (w0 * pltpu.roll(z, 2, axis=0) + w1 * pltpu.roll(z, 1, axis=0) + w2 * z)).astype(BF16)
        hb = BF16_SUBLANES
        first = (i % tiles_per_seq) == 0
        hz_prev = hc_ref[...].astype(F32) * hx_ref[...].astype(F32)
        hz_meta = mc_ref[...].astype(F32) * mx_ref[...].astype(F32)
        hz = jnp.where(first, hz_meta, hz_prev)
        zm1 = hz[hb - 1:hb]
        zm2 = hz[hb - 2:hb - 1]
        zh = z[0:hb]
        row = lax.broadcasted_iota(jnp.int32, zh.shape, 0)
        z1 = jnp.where(row == 0, zm1, pltpu.roll(zh, 1, axis=0))
        z2 = jnp.where(row == 0, zm2, jnp.where(row == 1, zm1, pltpu.roll(zh, 2, axis=0)))
        y_ref[0:hb, :] = (cb[0:hb] * (w0 * z2 + w1 * z1 + w2 * zh)).astype(BF16)

    pa = _dot(o_ref[...], wa_ref[...])
    pc = _dot(y_ref[...], wc_ref[...])
    out = _sigmoid(ga_ref[...].astype(F32)) * pa + _sigmoid(gc_ref[...].astype(F32)) * pc
    out_ref[...] = out.astype(BF16)


def _merge(o_n, u, u_meta, w_conv, wa_bf, wc_bf, *, seq, d, tm, tn):
    rows = o_n.shape[0]
    cd = d // 2
    hb = BF16_SUBLANES
    kern = functools.partial(_merge_kernel, tiles_per_seq=seq // tm)
    halo = lambda c: pl.BlockSpec((hb, cd), lambda i, j: (jnp.maximum(i * (tm // hb) - 1, 0), c))
    return pl.pallas_call(
        kern,
        grid=(rows // tm, d // tn),
        in_specs=[
            pl.BlockSpec((tm, cd), lambda i, j: (i, 0)),
            pl.BlockSpec((tm, cd), lambda i, j: (i, 3)),
            pl.BlockSpec((tm, cd), lambda i, j: (i, 4)),
            pl.BlockSpec((tm, cd), lambda i, j: (i, 5)),
            halo(3),
            halo(4),
            pl.BlockSpec((hb, cd), lambda i, j: (0, 3)),
            pl.BlockSpec((hb, cd), lambda i, j: (0, 4)),
            pl.BlockSpec((3, cd), lambda i, j: (0, 0)),
            pl.BlockSpec((tm, tn), lambda i, j: (i, 3 * d // tn + j)),
            pl.BlockSpec((tm, tn), lambda i, j: (i, 4 * d // tn + j)),
            pl.BlockSpec((cd, tn), lambda i, j: (0, j)),
            pl.BlockSpec((cd, tn), lambda i, j: (0, j)),
        ],
        out_specs=pl.BlockSpec((tm, tn), lambda i, j: (i, j)),
        out_shape=jax.ShapeDtypeStruct((rows, d), BF16),
        scratch_shapes=[pltpu.VMEM((tm, cd), BF16)],
        compiler_params=_params("arbitrary", "arbitrary"),
        name="merge",
    )(o_n, u, u, u, u, u, u_meta, u_meta, w_conv, u, u, wa_bf, wc_bf)


def _out_router_kernel(mg_ref, h_ref, wo_ref, g_ref, b_ref, wr_ref, rb_ref,
                       h1_ref, h1b_ref, wsel_ref, eidx_ref, *, alpha):
    m = _dot(mg_ref[...], wo_ref[...])
    h1 = _layer_norm(alpha * h_ref[...] + m, g_ref[...], b_ref[...])
    h1_ref[...] = h1
    h1b_ref[...] = h1.astype(BF16)

    logits = _dot_nt(wr_ref[...], h1, precision=lax.Precision.HIGHEST)
    s = _sigmoid(logits)
    sel = s + rb_ref[...]
    n_exp, tm = sel.shape
    per_group = n_exp // N_GROUPS
    neg = -jnp.inf

    grow = lax.broadcasted_iota(jnp.int32, (per_group, tm), 0).astype(F32)
    scores = []
    for g in range(N_GROUPS):
        sg = sel[g * per_group:(g + 1) * per_group]
        m1 = jnp.max(sg, axis=0, keepdims=True)
        first = jnp.min(jnp.where(sg == m1, grow, float(per_group)), axis=0, keepdims=True)
        m2 = jnp.max(jnp.where(grow == first, neg, sg), axis=0, keepdims=True)
        scores.append(m1 + m2)
    gs = jnp.concatenate(scores, axis=0)

    gidx = lax.broadcasted_iota(jnp.int32, gs.shape, 0)
    grank = jnp.zeros(gs.shape, F32)
    for g in range(N_GROUPS):
        o = gs[g:g + 1]
        grank = grank + jnp.where((o > gs) | ((o == gs) & (gidx > g)), 1.0, 0.0)
    gkeep = jnp.where(grank < TOPK_GROUPS, 1.0, 0.0)
    keep = jnp.concatenate([jnp.broadcast_to(gkeep[g:g + 1], (per_group, tm)) for g in range(N_GROUPS)], axis=0)
    selm = jnp.where(keep > 0.5, sel, neg)

    eidx = lax.broadcasted_iota(jnp.int32, selm.shape, 0)
    rank = jnp.zeros(selm.shape, F32)
    for e in range(n_exp):
        o = selm[e:e + 1]
        rank = rank + jnp.where((o > selm) | ((o == selm) & (eidx > e)), 1.0, 0.0)
    chosen = rank < TOP_K
    ssel = jnp.where(chosen, s, 0.0)
    w = ssel / jnp.sum(ssel, axis=0, keepdims=True) * ROUTED_SCALE
    wsel_ref[...] = jnp.where(chosen, w, -1.0)
    eidx_f = eidx.astype(F32)
    picks = [jnp.sum(jnp.where(rank == float(r), eidx_f, 0.0), axis=0, keepdims=True) for r in range(TOP_K)]
    eidx_ref[...] = jnp.concatenate(picks, axis=0).astype(jnp.int32)


def _out_router(merged, h, wo_bf, g, b, w_router_t, router_bias, *, tm, alpha):
    rows, d = merged.shape
    n_exp = w_router_t.shape[0]
    kern = functools.partial(_out_router_kernel, alpha=alpha)
    return pl.pallas_call(
        kern,
        grid=(rows // tm,),
        in_specs=[
            pl.BlockSpec((tm, d), lambda i: (i, 0)),
            pl.BlockSpec((tm, d), lambda i: (i, 0)),
            pl.BlockSpec((d, d), lambda i: (0, 0)),
            pl.BlockSpec((1, d), lambda i: (0, 0)),
            pl.BlockSpec((1, d), lambda i: (0, 0)),
            pl.BlockSpec((n_exp, d), lambda i: (0, 0)),
            pl.BlockSpec((n_exp, 1), lambda i: (0, 0)),
        ],
        out_specs=[
            pl.BlockSpec((tm, d), lambda i: (i, 0)),
            pl.BlockSpec((tm, d), lambda i: (i, 0)),
            pl.BlockSpec((n_exp, tm), lambda i: (0, i)),
            pl.BlockSpec((TOP_K, tm), lambda i: (0, i)),
        ],
        out_shape=[
            jax.ShapeDtypeStruct((rows, d), F32),
            jax.ShapeDtypeStruct((rows, d), BF16),
            jax.ShapeDtypeStruct((n_exp, rows), F32),
            jax.ShapeDtypeStruct((TOP_K, rows), jnp.int32),
        ],
        compiler_params=_params("arbitrary"),
        name="out_router",
    )(merged, h, wo_bf, g, b, w_router_t, router_bias)


def _expert_kernel(be_ref, nv_ref, xs_ref, wg_ref, wu_ref, wd_ref, ys_ref, wg_bf, wu_bf, wd_bf):
    b = pl.program_id(0)
    e = be_ref[b]
    prev = be_ref[jnp.maximum(b - 1, 0)]

    @pl.when((b == 0) | (e != prev))
    def _():
        wg_bf[...] = wg_ref[...].astype(BF16)
        wu_bf[...] = wu_ref[...].astype(BF16)
        wd_bf[...] = wd_ref[...].astype(BF16)

    @pl.when(b < nv_ref[0])
    def _():
        x = xs_ref[...]
        g = _dot(x, wg_bf[...])
        u = _dot(x, wu_bf[...])
        hid = g * _sigmoid(g) * u
        ys_ref[...] = _dot(hid.astype(BF16), wd_bf[...]).astype(BF16)

    @pl.when(b >= nv_ref[0])
    def _():
        ys_ref[...] = jnp.zeros_like(ys_ref)


def _experts(block_e, n_valid, xs, w_gate, w_up, w_down, *, bm):
    p_rows, d = xs.shape
    d_exp = w_gate.shape[-1]
    nb = p_rows // bm
    grid_spec = pltpu.PrefetchScalarGridSpec(
        num_scalar_prefetch=2,
        grid=(nb,),
        in_specs=[
            pl.BlockSpec((bm, d), lambda b, be, nv: (jnp.minimum(b, nv[0] - 1), 0)),
            pl.BlockSpec((None, d, d_exp), lambda b, be, nv: (be[b], 0, 0)),
            pl.BlockSpec((None, d, d_exp), lambda b, be, nv: (be[b], 0, 0)),
            pl.BlockSpec((None, d_exp, d), lambda b, be, nv: (be[b], 0, 0)),
        ],
        out_specs=pl.BlockSpec((bm, d), lambda b, be, nv: (b, 0)),
        scratch_shapes=[
            pltpu.VMEM((d, d_exp), BF16),
            pltpu.VMEM((d, d_exp), BF16),
            pltpu.VMEM((d_exp, d), BF16),
        ],
    )
    return pl.pallas_call(
        _expert_kernel,
        grid_spec=grid_spec,
        out_shape=jax.ShapeDtypeStruct((p_rows, d), BF16),
        compiler_params=_params("arbitrary"),
        name="experts",
    )(block_e, n_valid, xs, w_gate, w_up, w_down)


def _final_kernel(h1_ref, h1b_ref, r_ref, wg_ref, wu_ref, wd_ref, g_ref, b_ref, out_ref, *, alpha):
    x = h1b_ref[...]
    gate = _dot(x, wg_ref[...])
    up = _dot(x, wu_ref[...])
    hid = gate * _sigmoid(gate) * up
    shared = _dot(hid.astype(BF16), wd_ref[...])
    out_ref[...] = _layer_norm(alpha * h1_ref[...] + shared + r_ref[...], g_ref[...], b_ref[...])


def _final(h1, h1b, routed, wg_bf, wu_bf, wd_bf, g, b, *, tm, alpha):
    rows, d = h1.shape
    d_exp = wg_bf.shape[1]
    kern = functools.partial(_final_kernel, alpha=alpha)
    return pl.pallas_call(
        kern,
        grid=(rows // tm,),
        in_specs=[
            pl.BlockSpec((tm, d), lambda i: (i, 0)),
            pl.BlockSpec((tm, d), lambda i: (i, 0)),
            pl.BlockSpec((tm, d), lambda i: (i, 0)),
            pl.BlockSpec((d, d_exp), lambda i: (0, 0)),
            pl.BlockSpec((d, d_exp), lambda i: (0, 0)),
            pl.BlockSpec((d_exp, d), lambda i: (0, 0)),
            pl.BlockSpec((1, d), lambda i: (0, 0)),
            pl.BlockSpec((1, d), lambda i: (0, 0)),
        ],
        out_specs=pl.BlockSpec((tm, d), lambda i: (i, 0)),
        out_shape=jax.ShapeDtypeStruct((rows, d), F32),
        compiler_params=_params("arbitrary"),
        name="final",
    )(h1, h1b, routed, wg_bf, wu_bf, wd_bf, g, b)


def _rope_tables(pos, dh):
    half = dh // 2
    inv_freq = ROPE_THETA ** (-jnp.arange(half, dtype=F32) / half)
    ang = pos.astype(F32)[:, None] * inv_freq[None, :]
    cos = jnp.cos(ang)
    sin = jnp.sin(ang)
    reps = LANES // dh
    cos_full = jnp.tile(jnp.concatenate([cos, cos], axis=1), (1, reps))
    sin_full = jnp.tile(jnp.concatenate([-sin, sin], axis=1), (1, reps))
    return cos_full, sin_full


def _tile(n, pref):
    t = min(n, pref)
    assert n % t == 0, (n, pref)
    return t


def kernel(x, meta_tokens, ln0_g, ln0_b, w_in, lambda_q1, lambda_k1, lambda_q2, lambda_k2, subln_g, w_conv, w_proj_attn, w_proj_conv, w_out, ln1_g, ln1_b, w_router, router_bias, w_exp_gate, w_exp_up, w_exp_down, w_sh_gate, w_sh_up, w_sh_down, ln2_g, ln2_b):
    batch, seq, d = x.shape
    n_meta = meta_tokens.shape[0]
    depth = w_in.shape[0]
    dh = lambda_q1.shape[-1]
    n_exp = w_router.shape[-1]
    assert depth == 1 and 2 * dh == LANES and n_meta == BF16_SUBLANES
    assert n_exp // N_GROUPS == 8 and subln_g.shape[-1] == 2 * dh
    cd = d // 2
    heads = cd // (2 * dh)
    n_tok = batch * seq
    alpha = float((2 * depth) ** 0.25)

    row = lambda a: a.reshape(1, -1).astype(F32)

    w_in_bf = w_in[0].astype(BF16)
    cos_m, sin_m = _rope_tables(jnp.arange(n_meta), dh)
    cos_r, sin_r = _rope_tables(jnp.arange(n_meta, n_meta + seq), dh)
    tn_in = _tile(cd, 1024)
    tm_in = _tile(seq, 512)
    inproj = functools.partial(_in_proj, g=row(ln0_g), b=row(ln0_b), w_bf=w_in_bf, tn=tn_in, qk_width=cd, dh=dh)
    u, h = inproj(x.reshape(n_tok, d), cos=cos_r, sin=sin_r, tm=tm_in)
    u_meta, _ = inproj(meta_tokens.astype(F32), cos=cos_m, sin=sin_m, tm=n_meta)

    lamv = jnp.stack([lambda_q1[0], lambda_k1[0], lambda_q2[0], lambda_k2[0]]).astype(F32)
    o_n = _attention(u, u_meta, lamv, row(subln_g[0]), batch=batch, seq=seq, heads=heads, dh=dh,
                     tq=_tile(seq, 256))

    merged = _merge(o_n, u, u_meta, w_conv[0].astype(F32), w_proj_attn[0].astype(BF16),
                    w_proj_conv[0].astype(BF16), seq=seq, d=d, tm=_tile(seq, 512), tn=_tile(d, 1024))

    h1, h1b, wsel, eidx = _out_router(
        merged, h, w_out[0].astype(BF16), row(ln1_g[0]), row(ln1_b[0]),
        w_router[0].T.astype(F32), router_bias[0].reshape(n_exp, 1).astype(F32),
        tm=_tile(seq, 256), alpha=alpha)

    bm = 256
    nb = -(-n_tok * TOP_K // bm) + n_exp
    mask = wsel >= 0.0
    counts = jnp.sum(mask, axis=1, dtype=jnp.int32)
    pos = jnp.cumsum(mask.astype(jnp.int32), axis=1) - 1
    padded = (counts + bm - 1) // bm * bm
    pad_end = jnp.cumsum(padded)
    pad_start = pad_end - padded
    dest = pad_start[:, None] + pos
    dest_tk = jnp.take_along_axis(dest, eidx, axis=0)
    w_tk = jnp.take_along_axis(wsel, eidx, axis=0)
    tok = jnp.broadcast_to(jnp.arange(n_tok, dtype=jnp.int32)[None], dest_tk.shape)
    row_tok = jnp.zeros((nb * bm,), jnp.int32).at[dest_tk.reshape(-1)].set(tok.reshape(-1))
    block_e = jnp.minimum(jnp.searchsorted(pad_end, jnp.arange(nb, dtype=jnp.int32) * bm, side='right'),
                          n_exp - 1).astype(jnp.int32)
    n_valid = (pad_end[-1:] // bm).astype(jnp.int32)

    xs = jnp.take(h1b, row_tok, axis=0)
    ys = _experts(block_e, n_valid, xs, w_exp_gate[0], w_exp_up[0], w_exp_down[0], bm=bm)
    yg = jnp.take(ys, dest_tk, axis=0).astype(F32)
    routed = jnp.sum(yg * w_tk[:, :, None], axis=0)

    out = _final(h1, h1b, routed, w_sh_gate[0].astype(BF16), w_sh_up[0].astype(BF16),
                 w_sh_down[0].astype(BF16), row(ln2_g[0]), row(ln2_b[0]), tm=_tile(seq, 256), alpha=alpha)
    return out.reshape(batch, seq, d)
```

```python
import functools

import jax
import jax.numpy as jnp
import numpy as np
from jax import lax
from jax.experimental import pallas as pl
from jax.experimental.pallas import tpu as pltpu

CHUNK = 64
ROPE_THETA = 10000.0
LN_EPS = 1e-5
TOP_K = 8
N_GROUPS = 8
TOPK_GROUPS = 4
ROUTED_SCALE = 2.5
LAM_INIT = 0.2
LOG2E = 1.4426950408889634

LANES = 128
BF16_SUBLANES = 16
VMEM_LIMIT_BYTES = 56 * 1024 * 1024

F32 = jnp.float32
BF16 = jnp.bfloat16


def _params(*sem):
    return pltpu.CompilerParams(dimension_semantics=sem, vmem_limit_bytes=VMEM_LIMIT_BYTES)


def _layer_norm(x, g, b):
    mu = jnp.mean(x, axis=-1, keepdims=True)
    xc = x - mu
    var = jnp.mean(xc * xc, axis=-1, keepdims=True)
    return xc * lax.rsqrt(var + LN_EPS) * g + b


def _dot(a, b):
    return jnp.dot(a, b, preferred_element_type=F32)


def _dot_nt(a, b, **kw):
    return lax.dot_general(a, b, (((1,), (1,)), ((), ())), preferred_element_type=F32, **kw)


def _sigmoid(x):
    return 1.0 / (1.0 + jnp.exp(-x))


def _inproj_kernel(x_ref, g_ref, b_ref, w_ref, cos_ref, sin_ref, u_ref, h_ref, hb_ref,
                   *, q_tiles, qk_scale, half):
    j = pl.program_id(1)

    @pl.when(j == 0)
    def _():
        h = _layer_norm(x_ref[...], g_ref[...], b_ref[...])
        h_ref[...] = h
        hb_ref[...] = h.astype(BF16)

    acc = _dot(hb_ref[...], w_ref[...])
    tn = acc.shape[1]

    @pl.when(j < 2 * q_tiles)
    def _():
        cos = cos_ref[...]
        sin = sin_ref[...]
        lane = lax.broadcasted_iota(jnp.int32, cos.shape, 1)
        first = (lane % (2 * half)) < half
        scale = jnp.where(j < q_tiles, qk_scale, 1.0).astype(F32)
        for c in range(tn // LANES):
            a = acc[:, c * LANES:(c + 1) * LANES]
            partner = jnp.where(first, pltpu.roll(a, LANES - half, axis=1), pltpu.roll(a, half, axis=1))
            u_ref[:, c * LANES:(c + 1) * LANES] = ((a * cos + partner * sin) * scale).astype(BF16)

    @pl.when(j >= 2 * q_tiles)
    def _():
        u_ref[...] = acc.astype(BF16)


def _in_proj(x2d, g, b, w_bf, cos, sin, *, tm, tn, qk_width, dh):
    rows, d = x2d.shape
    cols = w_bf.shape[1]
    pos_blocks = cos.shape[0] // tm
    kern = functools.partial(_inproj_kernel, q_tiles=qk_width // tn, qk_scale=float(dh) ** -0.5 * LOG2E, half=dh // 2)
    return pl.pallas_call(
        kern,
        grid=(rows // tm, cols // tn),
        in_specs=[
            pl.BlockSpec((tm, d), lambda i, j: (i, 0)),
            pl.BlockSpec((1, d), lambda i, j: (0, 0)),
            pl.BlockSpec((1, d), lambda i, j: (0, 0)),
            pl.BlockSpec((d, tn), lambda i, j: (0, j)),
            pl.BlockSpec((tm, LANES), lambda i, j: (i % pos_blocks, 0)),
            pl.BlockSpec((tm, LANES), lambda i, j: (i % pos_blocks, 0)),
        ],
        out_specs=[
            pl.BlockSpec((tm, tn), lambda i, j: (i, j)),
            pl.BlockSpec((tm, d), lambda i, j: (i, 0)),
        ],
        out_shape=[
            jax.ShapeDtypeStruct((rows, cols), BF16),
            jax.ShapeDtypeStruct((rows, d), F32),
        ],
        scratch_shapes=[pltpu.VMEM((tm, d), BF16)],
        compiler_params=_params("arbitrary", "arbitrary"),
        name="in_proj",
    )(x2d, g, b, w_bf, cos, sin)


def _attn_kernel(q_ref, k_ref, v_ref, km_ref, vm_ref, lamv_ref, g_ref, o_ref, vt_ref, vmt_ref, s_ref, *, tq, dh):
    qi = pl.program_id(2)
    seq = k_ref.shape[0]

    @pl.when(qi == 0)
    def _():
        for c in range(seq // tq):
            vt_ref[:, c * tq:(c + 1) * tq] = v_ref[c * tq:(c + 1) * tq, :].astype(F32).T.astype(BF16)
        vmt_ref[...] = vm_ref[...].astype(F32).T.astype(BF16)

    qt = q_ref[...].astype(F32).T
    dim = lax.broadcasted_iota(jnp.int32, qt.shape, 0)
    qq = jnp.concatenate([jnp.where(dim < dh, qt, 0.0), jnp.where(dim >= dh, qt, 0.0)], axis=1).astype(BF16)

    s = _dot(km_ref[...], qq)
    m = jnp.max(s, axis=0, keepdims=True)
    p = jnp.exp2(s - m)
    l = jnp.sum(p, axis=0, keepdims=True)
    acc = _dot(vmt_ref[...], p.astype(BF16))

    def update(carry, s, vt):
        m, l, acc = carry
        m_new = jnp.maximum(m, jnp.max(s, axis=0, keepdims=True))
        a = jnp.exp2(m - m_new)
        p = jnp.exp2(s - m_new)
        l = a * l + jnp.sum(p, axis=0, keepdims=True)
        acc = a * acc + _dot(vt, p.astype(BF16))
        return m_new, l, acc

    def scores(kb):
        return _dot(k_ref[pl.ds(pl.multiple_of(kb * tq, tq), tq), :], qq)

    def values(kb):
        return vt_ref[:, pl.ds(pl.multiple_of(kb * tq, tq), tq)]

    def block_pair(i, carry):
        s_ref[1] = scores(2 * i + 1)
        carry = update(carry, s_ref[0], values(2 * i))
        s_ref[0] = scores(2 * i + 2)
        return update(carry, s_ref[1], values(2 * i + 1))

    def odd_block(carry):
        s_ref[1] = scores(qi)
        carry = update(carry, s_ref[0], values(qi - 1))
        s_ref[0] = s_ref[1]
        return carry

    s_ref[0] = scores(0)
    carry = lax.fori_loop(0, qi // 2, block_pair, (m, l, acc))
    m, l, acc = lax.cond(qi % 2 == 1, odd_block, lambda c: c, carry)
    s = s_ref[0]

    start = pl.multiple_of(qi * tq, tq)
    key = lax.broadcasted_iota(jnp.int32, s.shape, 0)
    qry = lax.broadcasted_iota(jnp.int32, s.shape, 1)
    qry = jnp.where(qry >= tq, qry - tq, qry)
    s = jnp.where((key // CHUNK) <= (qry // CHUNK), s, -jnp.inf)
    m, l, acc = update((m, l, acc), s, vt_ref[:, pl.ds(start, tq)])

    lamv = lamv_ref[...]
    lam = (jnp.exp(jnp.sum(lamv[0:1] * lamv[1:2], axis=1, keepdims=True))
           - jnp.exp(jnp.sum(lamv[2:3] * lamv[3:4], axis=1, keepdims=True)) + LAM_INIT)
    o_all = acc / l
    o = (o_all[:, :tq] - lam * o_all[:, tq:]).T
    ms = jnp.mean(o * o, axis=-1, keepdims=True)
    o = o * lax.rsqrt(ms + LN_EPS) * g_ref[...] * (1.0 - LAM_INIT)
    o_ref[...] = o.astype(BF16)


def _attention(u, u_meta, lamv, subln_g, *, batch, seq, heads, dh, tq):
    vd = 2 * dh
    nq = seq // tq
    kern = functools.partial(_attn_kernel, tq=tq, dh=dh)
    n_meta = u_meta.shape[0]
    return pl.pallas_call(
        kern,
        grid=(batch, heads, nq),
        in_specs=[
            pl.BlockSpec((tq, vd), lambda b, h, i: (b * nq + i, h)),
            pl.BlockSpec((seq, vd), lambda b, h, i: (b, heads + h)),
            pl.BlockSpec((seq, vd), lambda b, h, i: (b, 2 * heads + h)),
            pl.BlockSpec((n_meta, vd), lambda b, h, i: (0, heads + h)),
            pl.BlockSpec((n_meta, vd), lambda b, h, i: (0, 2 * heads + h)),
            pl.BlockSpec((4, dh), lambda b, h, i: (0, 0)),
            pl.BlockSpec((1, vd), lambda b, h, i: (0, 0)),
        ],
        out_specs=pl.BlockSpec((tq, vd), lambda b, h, i: (b * nq + i, h)),
        out_shape=jax.ShapeDtypeStruct((batch * seq, heads * vd), BF16),
        scratch_shapes=[pltpu.VMEM((vd, seq), BF16), pltpu.VMEM((vd, n_meta), BF16),
                        pltpu.VMEM((2, tq, 2 * tq), F32)],
        compiler_params=_params("arbitrary", "arbitrary", "arbitrary"),
        name="attention",
    )(u, u, u, u_meta, u_meta, lamv, subln_g)


def _merge_kernel(o_ref, cx_ref, cc_ref, cb_ref, hx_ref, hc_ref, mx_ref, mc_ref, wconv_ref,
                  ga_ref, gc_ref, wa_ref, wc_ref, out_ref, y_ref, *, tiles_per_seq):
    i = pl.program_id(0)
    j = pl.program_id(1)

    @pl.when(j == 0)
    def _():
        z = cc_ref[...].astype(F32) * cx_ref[...].astype(F32)
        cb = cb_ref[...].astype(F32)
        w = wconv_ref[...]
        w0, w1, w2 = w[0:1], w[1:2], w[2:3]
        y_ref[...] = (cb * (w0 * pltpu.roll(z, 2, axis=0) + w1 * pltpu.roll(z, 1, axis=0) + w2 * z)).astype(BF16)
        hb = BF16_SUBLANES
        first = (i % tiles_per_seq) == 0
        hz_prev = hc_ref[...].astype(F32) * hx_ref[...].astype(F32)
        hz_meta = mc_ref[...].astype(F32) * mx_ref[...].astype(F32)
        hz = jnp.where(first, hz_meta, hz_prev)
        zm1 = hz[hb - 1:hb]
        zm2 = hz[hb - 2:hb - 1]
        zh = z[0:hb]
        row = lax.broadcasted_iota(jnp.int32, zh.shape, 0)
        z1 = jnp.where(row == 0, zm1, pltpu.roll(zh, 1, axis=0))
        z2 = jnp.where(row == 0, zm2, jnp.where(row == 1, zm1, pltpu.roll(zh, 2, axis=0)))
        y_ref[0:hb, :] = (cb[0:hb] * (w0 * z2 + w1 * z1 + w2 * zh)).astype(BF16)

    pa = _dot(o_ref[...], wa_ref[...])
    pc = _dot(y_ref[...], wc_ref[...])
    out = _sigmoid(ga_ref[...].astype(F32)) * pa + _sigmoid(gc_ref[...].astype(F32)) * pc
    out_ref[...] = out.astype(BF16)


def _merge(o_n, u, u_meta, w_conv, wa_bf, wc_bf, *, seq, d, tm, tn):
    rows = o_n.shape[0]
    cd = d // 2
    hb = BF16_SUBLANES
    kern = functools.partial(_merge_kernel, tiles_per_seq=seq // tm)
    halo = lambda c: pl.BlockSpec((hb, cd), lambda i, j: (jnp.maximum(i * (tm // hb) - 1, 0), c))
    return pl.pallas_call(
        kern,
        grid=(rows // tm, d // tn),
        in_specs=[
            pl.BlockSpec((tm, cd), lambda i, j: (i, 0)),
            pl.BlockSpec((tm, cd), lambda i, j: (i, 3)),
            pl.BlockSpec((tm, cd), lambda i, j: (i, 4)),
            pl.BlockSpec((tm, cd), lambda i, j: (i, 5)),
            halo(3),
            halo(4),
            pl.BlockSpec((hb, cd), lambda i, j: (0, 3)),
            pl.BlockSpec((hb, cd), lambda i, j: (0, 4)),
            pl.BlockSpec((3, cd), lambda i, j: (0, 0)),
            pl.BlockSpec((tm, tn), lambda i, j: (i, 3 * d // tn + j)),
            pl.BlockSpec((tm, tn), lambda i, j: (i, 4 * d // tn + j)),
            pl.BlockSpec((cd, tn), lambda i, j: (0, j)),
            pl.BlockSpec((cd, tn), lambda i, j: (0, j)),
        ],
        out_specs=pl.BlockSpec((tm, tn), lambda i, j: (i, j)),
        out_shape=jax.ShapeDtypeStruct((rows, d), BF16),
        scratch_shapes=[pltpu.VMEM((tm, cd), BF16)],
        compiler_params=_params("arbitrary", "arbitrary"),
        name="merge",
    )(o_n, u, u, u, u, u, u_meta, u_meta, w_conv, u, u, wa_bf, wc_bf)


def _out_router_kernel(mg_ref, h_ref, wo_ref, g_ref, b_ref, wr_ref, rb_ref, tri_ref,
                       h1_ref, h1b_ref, eidx_ref, pos_ref, wtk_ref, counts_ref, cnt_ref, *, alpha):
    @pl.when(pl.program_id(0) == 0)
    def _():
        cnt_ref[...] = jnp.zeros_like(cnt_ref)

    m = _dot(mg_ref[...], wo_ref[...])
    h1 = _layer_norm(alpha * h_ref[...] + m, g_ref[...], b_ref[...])
    h1_ref[...] = h1
    h1b_ref[...] = h1.astype(BF16)

    logits = _dot_nt(wr_ref[...], h1, precision=lax.Precision.HIGHEST)
    s = _sigmoid(logits)
    sel = s + rb_ref[...]
    n_exp, tm = sel.shape
    per_group = n_exp // N_GROUPS
    neg = -jnp.inf

    grow = lax.broadcasted_iota(jnp.int32, (per_group, tm), 0).astype(F32)
    scores = []
    for g in range(N_GROUPS):
        sg = sel[g * per_group:(g + 1) * per_group]
        m1 = jnp.max(sg, axis=0, keepdims=True)
        first = jnp.min(jnp.where(sg == m1, grow, float(per_group)), axis=0, keepdims=True)
        m2 = jnp.max(jnp.where(grow == first, neg, sg), axis=0, keepdims=True)
        scores.append(m1 + m2)
    gs = jnp.concatenate(scores, axis=0)

    gidx = lax.broadcasted_iota(jnp.int32, gs.shape, 0)
    grank = jnp.zeros(gs.shape, F32)
    for g in range(N_GROUPS):
        o = gs[g:g + 1]
        grank = grank + jnp.where((o > gs) | ((o == gs) & (gidx > g)), 1.0, 0.0)
    gkeep = jnp.where(grank < TOPK_GROUPS, 1.0, 0.0)
    keep = jnp.concatenate([jnp.broadcast_to(gkeep[g:g + 1], (per_group, tm)) for g in range(N_GROUPS)], axis=0)
    selm = jnp.where(keep > 0.5, sel, neg)

    eidx = lax.broadcasted_iota(jnp.int32, selm.shape, 0)
    rank = jnp.zeros(selm.shape, F32)
    for e in range(n_exp):
        o = selm[e:e + 1]
        rank = rank + jnp.where((o > selm) | ((o == selm) & (eidx > e)), 1.0, 0.0)
    chosen = rank < TOP_K
    ssel = jnp.where(chosen, s, 0.0)
    w = ssel / jnp.sum(ssel, axis=0, keepdims=True) * ROUTED_SCALE

    chosen_f = jnp.where(chosen, 1.0, 0.0)
    before = cnt_ref[...]
    pos = _dot(chosen_f.astype(BF16), tri_ref[...]) + before
    cnt_ref[...] = before + jnp.sum(chosen_f, axis=1, keepdims=True)
    counts_ref[...] = cnt_ref[...]

    def pick(v):
        rows = [jnp.sum(jnp.where(rank == float(r), v, 0.0), axis=0, keepdims=True) for r in range(TOP_K)]
        return jnp.concatenate(rows, axis=0)

    eidx_ref[...] = pick(eidx.astype(F32)).astype(jnp.int32)
    pos_ref[...] = pick(pos).astype(jnp.int32)
    wtk_ref[...] = pick(w)


def _out_router(merged, h, wo_bf, g, b, w_router_t, router_bias, *, tm, alpha):
    rows, d = merged.shape
    n_exp = w_router_t.shape[0]
    tri = jnp.triu(jnp.ones((tm, tm), BF16), 1)
    kern = functools.partial(_out_router_kernel, alpha=alpha)
    per_tok = lambda dt: jax.ShapeDtypeStruct((TOP_K, rows), dt)
    return pl.pallas_call(
        kern,
        grid=(rows // tm,),
        in_specs=[
            pl.BlockSpec((tm, d), lambda i: (i, 0)),
            pl.BlockSpec((tm, d), lambda i: (i, 0)),
            pl.BlockSpec((d, d), lambda i: (0, 0)),
            pl.BlockSpec((1, d), lambda i: (0, 0)),
            pl.BlockSpec((1, d), lambda i: (0, 0)),
            pl.BlockSpec((n_exp, d), lambda i: (0, 0)),
            pl.BlockSpec((n_exp, 1), lambda i: (0, 0)),
            pl.BlockSpec((tm, tm), lambda i: (0, 0)),
        ],
        out_specs=[
            pl.BlockSpec((tm, d), lambda i: (i, 0)),
            pl.BlockSpec((tm, d), lambda i: (i, 0)),
            pl.BlockSpec((TOP_K, tm), lambda i: (0, i)),
            pl.BlockSpec((TOP_K, tm), lambda i: (0, i)),
            pl.BlockSpec((TOP_K, tm), lambda i: (0, i)),
            pl.BlockSpec((n_exp, 1), lambda i: (0, 0)),
        ],
        out_shape=[
            jax.ShapeDtypeStruct((rows, d), F32),
            jax.ShapeDtypeStruct((rows, d), BF16),
            per_tok(jnp.int32),
            per_tok(jnp.int32),
            per_tok(F32),
            jax.ShapeDtypeStruct((n_exp, 1), F32),
        ],
        scratch_shapes=[pltpu.VMEM((n_exp, 1), F32)],
        compiler_params=_params("arbitrary"),
        name="out_router",
    )(merged, h, wo_bf, g, b, w_router_t, router_bias, tri)


def _expert_kernel(be_ref, nv_ref, xs_ref, wg_ref, wu_ref, wd_ref, ys_ref, wg_bf, wu_bf, wd_bf, slot_ref):
    s = pl.program_id(0)
    last = pl.num_programs(0) - 2
    cur = be_ref[jnp.maximum(s - 1, 0)]
    nxt = be_ref[jnp.minimum(s, last)]

    @pl.when(s == 0)
    def _():
        slot_ref[0] = 0

    slot = slot_ref[0]
    fill = jnp.where(s == 0, 0, 1 - slot)

    @pl.when((s == 0) | (nxt != cur))
    def _():
        wg_bf[fill] = wg_ref[...].astype(BF16)
        wu_bf[fill] = wu_ref[...].astype(BF16)
        wd_bf[fill] = wd_ref[...].astype(BF16)

    block = s - 1

    @pl.when((block >= 0) & (block < nv_ref[0]))
    def _():
        x = xs_ref[...]
        g = _dot(x, wg_bf[slot])
        u = _dot(x, wu_bf[slot])
        hid = g * _sigmoid(g) * u
        ys_ref[...] = _dot(hid.astype(BF16), wd_bf[slot]).astype(BF16)

    @pl.when(block >= nv_ref[0])
    def _():
        ys_ref[...] = jnp.zeros_like(ys_ref)

    @pl.when((s > 0) & (nxt != cur))
    def _():
        slot_ref[0] = 1 - slot


def _experts(block_e, n_valid, xs, w_gate, w_up, w_down, *, bm):
    p_rows, d = xs.shape
    d_exp = w_gate.shape[-1]
    nb = p_rows // bm
    ahead = lambda s, be, nv: (be[jnp.minimum(s, nb - 1)], 0, 0)
    grid_spec = pltpu.PrefetchScalarGridSpec(
        num_scalar_prefetch=2,
        grid=(nb + 1,),
        in_specs=[
            pl.BlockSpec((bm, d), lambda s, be, nv: (jnp.clip(s - 1, 0, nv[0] - 1), 0)),
            pl.BlockSpec((None, d, d_exp), ahead),
            pl.BlockSpec((None, d, d_exp), ahead),
            pl.BlockSpec((None, d_exp, d), ahead),
        ],
        out_specs=pl.BlockSpec((bm, d), lambda s, be, nv: (jnp.maximum(s - 1, 0), 0)),
        scratch_shapes=[
            pltpu.VMEM((2, d, d_exp), BF16),
            pltpu.VMEM((2, d, d_exp), BF16),
            pltpu.VMEM((2, d_exp, d), BF16),
            pltpu.SMEM((1,), jnp.int32),
        ],
    )
    return pl.pallas_call(
        _expert_kernel,
        grid_spec=grid_spec,
        out_shape=jax.ShapeDtypeStruct((p_rows, d), BF16),
        compiler_params=_params("arbitrary"),
        name="experts",
    )(block_e, n_valid, xs, w_gate, w_up, w_down)


def _final_kernel(h1_ref, h1b_ref, r_ref, wg_ref, wu_ref, wd_ref, g_ref, b_ref, out_ref, *, alpha):
    x = h1b_ref[...]
    gate = _dot(x, wg_ref[...])
    up = _dot(x, wu_ref[...])
    hid = gate * _sigmoid(gate) * up
    shared = _dot(hid.astype(BF16), wd_ref[...])
    out_ref[...] = _layer_norm(alpha * h1_ref[...] + shared + r_ref[...], g_ref[...], b_ref[...])


def _final(h1, h1b, routed, wg_bf, wu_bf, wd_bf, g, b, *, tm, alpha):
    rows, d = h1.shape
    d_exp = wg_bf.shape[1]
    kern = functools.partial(_final_kernel, alpha=alpha)
    return pl.pallas_call(
        kern,
        grid=(rows // tm,),
        in_specs=[
            pl.BlockSpec((tm, d), lambda i: (i, 0)),
            pl.BlockSpec((tm, d), lambda i: (i, 0)),
            pl.BlockSpec((tm, d), lambda i: (i, 0)),
            pl.BlockSpec((d, d_exp), lambda i: (0, 0)),
            pl.BlockSpec((d, d_exp), lambda i: (0, 0)),
            pl.BlockSpec((d_exp, d), lambda i: (0, 0)),
            pl.BlockSpec((1, d), lambda i: (0, 0)),
            pl.BlockSpec((1, d), lambda i: (0, 0)),
        ],
        out_specs=pl.BlockSpec((tm, d), lambda i: (i, 0)),
        out_shape=jax.ShapeDtypeStruct((rows, d), F32),
        compiler_params=_params("arbitrary"),
        name="final",
    )(h1, h1b, routed, wg_bf, wu_bf, wd_bf, g, b)


def _rope_tables(pos, dh):
    half = dh // 2
    inv_freq = ROPE_THETA ** (-jnp.arange(half, dtype=F32) / half)
    ang = pos.astype(F32)[:, None] * inv_freq[None, :]
    cos = jnp.cos(ang)
    sin = jnp.sin(ang)
    reps = LANES // dh
    cos_full = jnp.tile(jnp.concatenate([cos, cos], axis=1), (1, reps))
    sin_full = jnp.tile(jnp.concatenate([-sin, sin], axis=1), (1, reps))
    return cos_full, sin_full


def _tile(n, pref):
    t = min(n, pref)
    assert n % t == 0, (n, pref)
    return t


def kernel(x, meta_tokens, ln0_g, ln0_b, w_in, lambda_q1, lambda_k1, lambda_q2, lambda_k2, subln_g, w_conv, w_proj_attn, w_proj_conv, w_out, ln1_g, ln1_b, w_router, router_bias, w_exp_gate, w_exp_up, w_exp_down, w_sh_gate, w_sh_up, w_sh_down, ln2_g, ln2_b):
    batch, seq, d = x.shape
    n_meta = meta_tokens.shape[0]
    depth = w_in.shape[0]
    dh = lambda_q1.shape[-1]
    n_exp = w_router.shape[-1]
    assert depth == 1 and 2 * dh == LANES and n_meta == BF16_SUBLANES
    assert n_exp // N_GROUPS == 8 and subln_g.shape[-1] == 2 * dh
    cd = d // 2
    heads = cd // (2 * dh)
    n_tok = batch * seq
    alpha = float((2 * depth) ** 0.25)

    row = lambda a: a.reshape(1, -1).astype(F32)

    w_in_bf = w_in[0].astype(BF16)
    cos_m, sin_m = _rope_tables(jnp.arange(n_meta), dh)
    cos_r, sin_r = _rope_tables(jnp.arange(n_meta, n_meta + seq), dh)
    tn_in = _tile(cd, 1024)
    tm_in = _tile(seq, 512)
    inproj = functools.partial(_in_proj, g=row(ln0_g), b=row(ln0_b), w_bf=w_in_bf, tn=tn_in, qk_width=cd, dh=dh)
    u, h = inproj(x.reshape(n_tok, d), cos=cos_r, sin=sin_r, tm=tm_in)
    u_meta, _ = inproj(meta_tokens.astype(F32), cos=cos_m, sin=sin_m, tm=n_meta)

    lamv = jnp.stack([lambda_q1[0], lambda_k1[0], lambda_q2[0], lambda_k2[0]]).astype(F32)
    o_n = _attention(u, u_meta, lamv, row(subln_g[0]), batch=batch, seq=seq, heads=heads, dh=dh,
                     tq=_tile(seq, 512))

    merged = _merge(o_n, u, u_meta, w_conv[0].astype(F32), w_proj_attn[0].astype(BF16),
                    w_proj_conv[0].astype(BF16), seq=seq, d=d, tm=_tile(seq, 512), tn=_tile(d, 1024))

    h1, h1b, eidx, pos_tk, w_tk, counts = _out_router(
        merged, h, w_out[0].astype(BF16), row(ln1_g[0]), row(ln1_b[0]),
        w_router[0].T.astype(F32), router_bias[0].reshape(n_exp, 1).astype(F32),
        tm=_tile(seq, 256), alpha=alpha)

    bm = 256
    nb = -(-n_tok * TOP_K // bm) + n_exp
    counts = counts[:, 0].astype(jnp.int32)
    padded = (counts + bm - 1) // bm * bm
    pad_end = jnp.cumsum(padded)
    pad_start = pad_end - padded
    experts = jnp.arange(n_exp, dtype=jnp.int32)
    dest_tk = pos_tk + jnp.sum(jnp.where(eidx[..., None] == experts, pad_start, 0), axis=-1)
    tok = jnp.broadcast_to(jnp.arange(n_tok, dtype=jnp.int32)[None], dest_tk.shape)
    row_tok = jnp.zeros((nb * bm,), jnp.int32).at[dest_tk.reshape(-1)].set(
        tok.reshape(-1), mode="promise_in_bounds", unique_indices=True)
    block_row = jnp.arange(nb, dtype=jnp.int32) * bm
    block_e = jnp.minimum(jnp.sum(pad_end[None, :] <= block_row[:, None], axis=1), n_exp - 1).astype(jnp.int32)
    n_valid = (pad_end[-1:] // bm).astype(jnp.int32)

    xs = h1b.at[row_tok].get(mode="promise_in_bounds")
    ys = _experts(block_e, n_valid, xs, w_exp_gate[0], w_exp_up[0], w_exp_down[0], bm=bm)
    yg = ys.at[dest_tk].get(mode="promise_in_bounds").astype(F32)
    routed = jnp.sum(yg * w_tk[:, :, None], axis=0)

    out = _final(h1, h1b, routed, w_sh_gate[0].astype(BF16), w_sh_up[0].astype(BF16),
                 w_sh_down[0].astype(BF16), row(ln2_g[0]), row(ln2_b[0]), tm=_tile(seq, 256), alpha=alpha)
    return out.reshape(batch, seq, d)
```

```python
import functools

import jax
import jax.numpy as jnp
import numpy as np
from jax import lax
from jax.experimental import pallas as pl
from jax.experimental.pallas import tpu as pltpu

CHUNK = 64
ROPE_THETA = 10000.0
LN_EPS = 1e-5
TOP_K = 8
N_GROUPS = 8
TOPK_GROUPS = 4
ROUTED_SCALE = 2.5
LAM_INIT = 0.2
LOG2E = 1.4426950408889634

LANES = 128
BF16_SUBLANES = 16
VMEM_LIMIT_BYTES = 56 * 1024 * 1024

F32 = jnp.float32
BF16 = jnp.bfloat16


def _params(*sem):
    return pltpu.CompilerParams(dimension_semantics=sem, vmem_limit_bytes=VMEM_LIMIT_BYTES)


def _layer_norm(x, g, b):
    mu = jnp.mean(x, axis=-1, keepdims=True)
    xc = x - mu
    var = jnp.mean(xc * xc, axis=-1, keepdims=True)
    return xc * lax.rsqrt(var + LN_EPS) * g + b


def _dot(a, b):
    return jnp.dot(a, b, preferred_element_type=F32)


def _dot_nt(a, b, **kw):
    return lax.dot_general(a, b, (((1,), (1,)), ((), ())), preferred_element_type=F32, **kw)


def _sigmoid(x):
    return 1.0 / (1.0 + jnp.exp(-x))


def _store_slabs(ref, x):
    rows, d = x.shape
    n_slab = d // LANES
    for c in range(n_slab):
        ref[pl.ds(c, rows, stride=n_slab), :] = x[:, c * LANES:(c + 1) * LANES]


def _load_slabs(ref, rows, d):
    n_slab = d // LANES
    return jnp.concatenate([ref[pl.ds(c, rows, stride=n_slab), :] for c in range(n_slab)], axis=1)


def _inproj_kernel(x_ref, g_ref, b_ref, w_ref, cos_ref, sin_ref, u_ref, h_ref, hb_ref,
                   *, q_tiles, qk_scale, half):
    j = pl.program_id(1)

    @pl.when(j == 0)
    def _():
        h = _layer_norm(x_ref[...], g_ref[...], b_ref[...])
        h_ref[...] = h
        hb_ref[...] = h.astype(BF16)

    acc = _dot(hb_ref[...], w_ref[...])
    tn = acc.shape[1]

    @pl.when(j < 2 * q_tiles)
    def _():
        cos = cos_ref[...]
        sin = sin_ref[...]
        lane = lax.broadcasted_iota(jnp.int32, cos.shape, 1)
        first = (lane % (2 * half)) < half
        scale = jnp.where(j < q_tiles, qk_scale, 1.0).astype(F32)
        for c in range(tn // LANES):
            a = acc[:, c * LANES:(c + 1) * LANES]
            partner = jnp.where(first, pltpu.roll(a, LANES - half, axis=1), pltpu.roll(a, half, axis=1))
            u_ref[:, c * LANES:(c + 1) * LANES] = ((a * cos + partner * sin) * scale).astype(BF16)

    @pl.when(j >= 2 * q_tiles)
    def _():
        u_ref[...] = acc.astype(BF16)


def _in_proj(x2d, g, b, w_bf, cos, sin, *, tm, tn, qk_width, dh):
    rows, d = x2d.shape
    cols = w_bf.shape[1]
    pos_blocks = cos.shape[0] // tm
    kern = functools.partial(_inproj_kernel, q_tiles=qk_width // tn, qk_scale=float(dh) ** -0.5 * LOG2E, half=dh // 2)
    return pl.pallas_call(
        kern,
        grid=(rows // tm, cols // tn),
        in_specs=[
            pl.BlockSpec((tm, d), lambda i, j: (i, 0)),
            pl.BlockSpec((1, d), lambda i, j: (0, 0)),
            pl.BlockSpec((1, d), lambda i, j: (0, 0)),
            pl.BlockSpec((d, tn), lambda i, j: (0, j)),
            pl.BlockSpec((tm, LANES), lambda i, j: (i % pos_blocks, 0)),
            pl.BlockSpec((tm, LANES), lambda i, j: (i % pos_blocks, 0)),
        ],
        out_specs=[
            pl.BlockSpec((tm, tn), lambda i, j: (i, j)),
            pl.BlockSpec((tm, d), lambda i, j: (i, 0)),
        ],
        out_shape=[
            jax.ShapeDtypeStruct((rows, cols), BF16),
            jax.ShapeDtypeStruct((rows, d), F32),
        ],
        scratch_shapes=[pltpu.VMEM((tm, d), BF16)],
        compiler_params=_params("arbitrary", "arbitrary"),
        name="in_proj",
    )(x2d, g, b, w_bf, cos, sin)


def _attn_kernel(q_ref, k_ref, v_ref, km_ref, vm_ref, lamv_ref, g_ref, o_ref, vt_ref, vmt_ref, s_ref, *, tq, dh):
    qi = pl.program_id(2)
    seq = k_ref.shape[0]

    @pl.when(qi == 0)
    def _():
        for c in range(seq // tq):
            vt_ref[:, c * tq:(c + 1) * tq] = v_ref[c * tq:(c + 1) * tq, :].astype(F32).T.astype(BF16)
        vmt_ref[...] = vm_ref[...].astype(F32).T.astype(BF16)

    qt = q_ref[...].astype(F32).T
    dim = lax.broadcasted_iota(jnp.int32, qt.shape, 0)
    qq = jnp.concatenate([jnp.where(dim < dh, qt, 0.0), jnp.where(dim >= dh, qt, 0.0)], axis=1).astype(BF16)

    s = _dot(km_ref[...], qq)
    m = jnp.max(s, axis=0, keepdims=True)
    p = jnp.exp2(s - m)
    l = jnp.sum(p, axis=0, keepdims=True)
    acc = _dot(vmt_ref[...], p.astype(BF16))

    def update(carry, s, vt):
        m, l, acc = carry
        m_new = jnp.maximum(m, jnp.max(s, axis=0, keepdims=True))
        a = jnp.exp2(m - m_new)
        p = jnp.exp2(s - m_new)
        l = a * l + jnp.sum(p, axis=0, keepdims=True)
        acc = a * acc + _dot(vt, p.astype(BF16))
        return m_new, l, acc

    def scores(kb):
        return _dot(k_ref[pl.ds(pl.multiple_of(kb * tq, tq), tq), :], qq)

    def values(kb):
        return vt_ref[:, pl.ds(pl.multiple_of(kb * tq, tq), tq)]

    def block_pair(i, carry):
        s_ref[1] = scores(2 * i + 1)
        carry = update(carry, s_ref[0], values(2 * i))
        s_ref[0] = scores(2 * i + 2)
        return update(carry, s_ref[1], values(2 * i + 1))

    def odd_block(carry):
        s_ref[1] = scores(qi)
        carry = update(carry, s_ref[0], values(qi - 1))
        s_ref[0] = s_ref[1]
        return carry

    s_ref[0] = scores(0)
    carry = lax.fori_loop(0, qi // 2, block_pair, (m, l, acc))
    m, l, acc = lax.cond(qi % 2 == 1, odd_block, lambda c: c, carry)
    s = s_ref[0]

    start = pl.multiple_of(qi * tq, tq)
    key = lax.broadcasted_iota(jnp.int32, s.shape, 0)
    qry = lax.broadcasted_iota(jnp.int32, s.shape, 1)
    qry = jnp.where(qry >= tq, qry - tq, qry)
    s = jnp.where((key // CHUNK) <= (qry // CHUNK), s, -jnp.inf)
    m, l, acc = update((m, l, acc), s, vt_ref[:, pl.ds(start, tq)])

    lamv = lamv_ref[...]
    lam = (jnp.exp(jnp.sum(lamv[0:1] * lamv[1:2], axis=1, keepdims=True))
           - jnp.exp(jnp.sum(lamv[2:3] * lamv[3:4], axis=1, keepdims=True)) + LAM_INIT)
    o_all = acc / l
    o = (o_all[:, :tq] - lam * o_all[:, tq:]).T
    ms = jnp.mean(o * o, axis=-1, keepdims=True)
    o = o * lax.rsqrt(ms + LN_EPS) * g_ref[...] * (1.0 - LAM_INIT)
    o_ref[...] = o.astype(BF16)


def _attention(u, u_meta, lamv, subln_g, *, batch, seq, heads, dh, tq):
    vd = 2 * dh
    nq = seq // tq
    kern = functools.partial(_attn_kernel, tq=tq, dh=dh)
    n_meta = u_meta.shape[0]
    return pl.pallas_call(
        kern,
        grid=(batch, heads, nq),
        in_specs=[
            pl.BlockSpec((tq, vd), lambda b, h, i: (b * nq + i, h)),
            pl.BlockSpec((seq, vd), lambda b, h, i: (b, heads + h)),
            pl.BlockSpec((seq, vd), lambda b, h, i: (b, 2 * heads + h)),
            pl.BlockSpec((n_meta, vd), lambda b, h, i: (0, heads + h)),
            pl.BlockSpec((n_meta, vd), lambda b, h, i: (0, 2 * heads + h)),
            pl.BlockSpec((4, dh), lambda b, h, i: (0, 0)),
            pl.BlockSpec((1, vd), lambda b, h, i: (0, 0)),
        ],
        out_specs=pl.BlockSpec((tq, vd), lambda b, h, i: (b * nq + i, h)),
        out_shape=jax.ShapeDtypeStruct((batch * seq, heads * vd), BF16),
        scratch_shapes=[pltpu.VMEM((vd, seq), BF16), pltpu.VMEM((vd, n_meta), BF16),
                        pltpu.VMEM((2, tq, 2 * tq), F32)],
        compiler_params=_params("arbitrary", "arbitrary", "arbitrary"),
        name="attention",
    )(u, u, u, u_meta, u_meta, lamv, subln_g)


def _merge_kernel(o_ref, cx_ref, cc_ref, cb_ref, hx_ref, hc_ref, mx_ref, mc_ref, wconv_ref,
                  ga_ref, gc_ref, wa_ref, wc_ref, out_ref, y_ref, *, tiles_per_seq):
    i = pl.program_id(0)
    j = pl.program_id(1)

    @pl.when(j == 0)
    def _():
        z = cc_ref[...].astype(F32) * cx_ref[...].astype(F32)
        cb = cb_ref[...].astype(F32)
        w = wconv_ref[...]
        w0, w1, w2 = w[0:1], w[1:2], w[2:3]
        y_ref[...] = (cb * (w0 * pltpu.roll(z, 2, axis=0) + w1 * pltpu.roll(z, 1, axis=0) + w2 * z)).astype(BF16)
        hb = BF16_SUBLANES
        first = (i % tiles_per_seq) == 0
        hz_prev = hc_ref[...].astype(F32) * hx_ref[...].astype(F32)
        hz_meta = mc_ref[...].astype(F32) * mx_ref[...].astype(F32)
        hz = jnp.where(first, hz_meta, hz_prev)
        zm1 = hz[hb - 1:hb]
        zm2 = hz[hb - 2:hb - 1]
        zh = z[0:hb]
        row = lax.broadcasted_iota(jnp.int32, zh.shape, 0)
        z1 = jnp.where(row == 0, zm1, pltpu.roll(zh, 1, axis=0))
        z2 = jnp.where(row == 0, zm2, jnp.where(row == 1, zm1, pltpu.roll(zh, 2, axis=0)))
        y_ref[0:hb, :] = (cb[0:hb] * (w0 * z2 + w1 * z1 + w2 * zh)).astype(BF16)

    pa = _dot(o_ref[...], wa_ref[...])
    pc = _dot(y_ref[...], wc_ref[...])
    out = _sigmoid(ga_ref[...].astype(F32)) * pa + _sigmoid(gc_ref[...].astype(F32)) * pc
    out_ref[...] = out.astype(BF16)


def _merge(o_n, u, u_meta, w_conv, wa_bf, wc_bf, *, seq, d, tm, tn):
    rows = o_n.shape[0]
    cd = d // 2
    hb = BF16_SUBLANES
    kern = functools.partial(_merge_kernel, tiles_per_seq=seq // tm)
    halo = lambda c: pl.BlockSpec((hb, cd), lambda i, j: (jnp.maximum(i * (tm // hb) - 1, 0), c))
    return pl.pallas_call(
        kern,
        grid=(rows // tm, d // tn),
        in_specs=[
            pl.BlockSpec((tm, cd), lambda i, j: (i, 0)),
            pl.BlockSpec((tm, cd), lambda i, j: (i, 3)),
            pl.BlockSpec((tm, cd), lambda i, j: (i, 4)),
            pl.BlockSpec((tm, cd), lambda i, j: (i, 5)),
            halo(3),
            halo(4),
            pl.BlockSpec((hb, cd), lambda i, j: (0, 3)),
            pl.BlockSpec((hb, cd), lambda i, j: (0, 4)),
            pl.BlockSpec((3, cd), lambda i, j: (0, 0)),
            pl.BlockSpec((tm, tn), lambda i, j: (i, 3 * d // tn + j)),
            pl.BlockSpec((tm, tn), lambda i, j: (i, 4 * d // tn + j)),
            pl.BlockSpec((cd, tn), lambda i, j: (0, j)),
            pl.BlockSpec((cd, tn), lambda i, j: (0, j)),
        ],
        out_specs=pl.BlockSpec((tm, tn), lambda i, j: (i, j)),
        out_shape=jax.ShapeDtypeStruct((rows, d), BF16),
        scratch_shapes=[pltpu.VMEM((tm, cd), BF16)],
        compiler_params=_params("arbitrary", "arbitrary"),
        name="merge",
    )(o_n, u, u, u, u, u, u_meta, u_meta, w_conv, u, u, wa_bf, wc_bf)


def _out_router_kernel(mg_ref, h_ref, wo_ref, g_ref, b_ref, wr_ref, rb_ref, tri_ref,
                       h1_ref, h1s_ref, eidx_ref, pos_ref, wtk_ref, counts_ref, cnt_ref, *, alpha):
    @pl.when(pl.program_id(0) == 0)
    def _():
        cnt_ref[...] = jnp.zeros_like(cnt_ref)

    m = _dot(mg_ref[...], wo_ref[...])
    h1 = _layer_norm(alpha * h_ref[...] + m, g_ref[...], b_ref[...])
    h1_ref[...] = h1
    _store_slabs(h1s_ref, h1)

    logits = _dot_nt(wr_ref[...], h1, precision=lax.Precision.HIGHEST)
    s = _sigmoid(logits)
    sel = s + rb_ref[...]
    n_exp, tm = sel.shape
    per_group = n_exp // N_GROUPS
    neg = -jnp.inf

    grow = lax.broadcasted_iota(jnp.int32, (per_group, tm), 0).astype(F32)
    scores = []
    for g in range(N_GROUPS):
        sg = sel[g * per_group:(g + 1) * per_group]
        m1 = jnp.max(sg, axis=0, keepdims=True)
        first = jnp.min(jnp.where(sg == m1, grow, float(per_group)), axis=0, keepdims=True)
        m2 = jnp.max(jnp.where(grow == first, neg, sg), axis=0, keepdims=True)
        scores.append(m1 + m2)
    gs = jnp.concatenate(scores, axis=0)

    gidx = lax.broadcasted_iota(jnp.int32, gs.shape, 0)
    grank = jnp.zeros(gs.shape, F32)
    for g in range(N_GROUPS):
        o = gs[g:g + 1]
        grank = grank + jnp.where((o > gs) | ((o == gs) & (gidx > g)), 1.0, 0.0)
    gkeep = jnp.where(grank < TOPK_GROUPS, 1.0, 0.0)
    keep = jnp.concatenate([jnp.broadcast_to(gkeep[g:g + 1], (per_group, tm)) for g in range(N_GROUPS)], axis=0)
    selm = jnp.where(keep > 0.5, sel, neg)

    eidx = lax.broadcasted_iota(jnp.int32, selm.shape, 0)
    rank = jnp.zeros(selm.shape, F32)
    for e in range(n_exp):
        o = selm[e:e + 1]
        rank = rank + jnp.where((o > selm) | ((o == selm) & (eidx > e)), 1.0, 0.0)
    chosen = rank < TOP_K
    ssel = jnp.where(chosen, s, 0.0)
    w = ssel / jnp.sum(ssel, axis=0, keepdims=True) * ROUTED_SCALE

    chosen_f = jnp.where(chosen, 1.0, 0.0)
    before = cnt_ref[...]
    pos = _dot(chosen_f.astype(BF16), tri_ref[...]) + before
    cnt_ref[...] = before + jnp.sum(chosen_f, axis=1, keepdims=True)
    counts_ref[...] = cnt_ref[...]

    def pick(v):
        rows = [jnp.sum(jnp.where(rank == float(r), v, 0.0), axis=0, keepdims=True) for r in range(TOP_K)]
        return jnp.concatenate(rows, axis=0)

    eidx_ref[...] = pick(eidx.astype(F32)).astype(jnp.int32)
    pos_ref[...] = pick(pos).astype(jnp.int32)
    wtk_ref[...] = pick(w)


def _out_router(merged, h, wo_bf, g, b, w_router_t, router_bias, *, tm, alpha, row0, rows):
    d = merged.shape[1]
    first = row0 // tm
    n_exp = w_router_t.shape[0]
    n_slab = d // LANES
    tri = jnp.triu(jnp.ones((tm, tm), BF16), 1)
    kern = functools.partial(_out_router_kernel, alpha=alpha)
    per_tok = lambda dt: jax.ShapeDtypeStruct((TOP_K, rows), dt)
    return pl.pallas_call(
        kern,
        grid=(rows // tm,),
        in_specs=[
            pl.BlockSpec((tm, d), lambda i: (first + i, 0)),
            pl.BlockSpec((tm, d), lambda i: (first + i, 0)),
            pl.BlockSpec((d, d), lambda i: (0, 0)),
            pl.BlockSpec((1, d), lambda i: (0, 0)),
            pl.BlockSpec((1, d), lambda i: (0, 0)),
            pl.BlockSpec((n_exp, d), lambda i: (0, 0)),
            pl.BlockSpec((n_exp, 1), lambda i: (0, 0)),
            pl.BlockSpec((tm, tm), lambda i: (0, 0)),
        ],
        out_specs=[
            pl.BlockSpec((tm, d), lambda i: (i, 0)),
            pl.BlockSpec((tm * n_slab, LANES), lambda i: (i, 0)),
            pl.BlockSpec((TOP_K, tm), lambda i: (0, i)),
            pl.BlockSpec((TOP_K, tm), lambda i: (0, i)),
            pl.BlockSpec((TOP_K, tm), lambda i: (0, i)),
            pl.BlockSpec((n_exp, 1), lambda i: (0, 0)),
        ],
        out_shape=[
            jax.ShapeDtypeStruct((rows, d), F32),
            jax.ShapeDtypeStruct((rows * n_slab, LANES), F32),
            per_tok(jnp.int32),
            per_tok(jnp.int32),
            per_tok(F32),
            jax.ShapeDtypeStruct((n_exp, 1), F32),
        ],
        scratch_shapes=[pltpu.VMEM((n_exp, 1), F32)],
        compiler_params=_params("arbitrary"),
        name="out_router",
    )(merged, h, wo_bf, g, b, w_router_t, router_bias, tri)


def _expert_kernel(be_ref, nv_ref, xs_ref, wg_ref, wu_ref, wd_ref, ys_ref, wg_bf, wu_bf, wd_bf, slot_ref):
    s = pl.program_id(0)
    last = pl.num_programs(0) - 2
    cur = be_ref[jnp.maximum(s - 1, 0)]
    nxt = be_ref[jnp.minimum(s, last)]

    @pl.when(s == 0)
    def _():
        slot_ref[0] = 0

    slot = slot_ref[0]
    fill = jnp.where(s == 0, 0, 1 - slot)

    @pl.when((s == 0) | (nxt != cur))
    def _():
        wg_bf[fill] = wg_ref[...].astype(BF16)
        wu_bf[fill] = wu_ref[...].astype(BF16)
        wd_bf[fill] = wd_ref[...].astype(BF16)

    block = s - 1

    @pl.when((block >= 0) & (block < nv_ref[0]))
    def _():
        d, d_exp = wg_ref.shape
        x = _load_slabs(xs_ref, xs_ref.shape[0] * LANES // d, d).astype(BF16)
        g = _dot(x, wg_bf[slot])
        u = _dot(x, wu_bf[slot])
        hid = g * _sigmoid(g) * u
        _store_slabs(ys_ref, _dot(hid.astype(BF16), wd_bf[slot]))

    @pl.when(block >= nv_ref[0])
    def _():
        ys_ref[...] = jnp.zeros_like(ys_ref)

    @pl.when((s > 0) & (nxt != cur))
    def _():
        slot_ref[0] = 1 - slot


def _experts(block_e, n_valid, xs, w_gate, w_up, w_down, *, bm):
    _, d, d_exp = w_gate.shape
    n_slab = d // LANES
    nb = xs.shape[0] // (bm * n_slab)
    ahead = lambda s, be, nv: (be[jnp.minimum(s, nb - 1)], 0, 0)
    grid_spec = pltpu.PrefetchScalarGridSpec(
        num_scalar_prefetch=2,
        grid=(nb + 1,),
        in_specs=[
            pl.BlockSpec((bm * n_slab, LANES), lambda s, be, nv: (jnp.maximum(jnp.minimum(s - 1, nv[0] - 1), 0), 0)),
            pl.BlockSpec((None, d, d_exp), ahead),
            pl.BlockSpec((None, d, d_exp), ahead),
            pl.BlockSpec((None, d_exp, d), ahead),
        ],
        out_specs=pl.BlockSpec((bm * n_slab, LANES), lambda s, be, nv: (jnp.maximum(s - 1, 0), 0)),
        scratch_shapes=[
            pltpu.VMEM((2, d, d_exp), BF16),
            pltpu.VMEM((2, d, d_exp), BF16),
            pltpu.VMEM((2, d_exp, d), BF16),
            pltpu.SMEM((1,), jnp.int32),
        ],
    )
    return pl.pallas_call(
        _expert_kernel,
        grid_spec=grid_spec,
        out_shape=jax.ShapeDtypeStruct(xs.shape, F32),
        compiler_params=_params("arbitrary"),
        name="experts",
    )(block_e, n_valid, xs, w_gate, w_up, w_down)


def _final_kernel(*refs, alpha):
    h1_ref = refs[0]
    y_refs = refs[1:1 + TOP_K]
    w_ref, wg_ref, wu_ref, wd_ref, g_ref, b_ref = refs[1 + TOP_K:7 + TOP_K]
    out_ref = refs[-1]
    h1 = h1_ref[...]
    tm, d = h1.shape
    x = h1.astype(BF16)
    gate = _dot(x, wg_ref[...])
    up = _dot(x, wu_ref[...])
    hid = gate * _sigmoid(gate) * up
    f = _dot(hid.astype(BF16), wd_ref[...])
    wt = w_ref[...].T
    for r in range(TOP_K):
        f = f + wt[:, r:r + 1] * _load_slabs(y_refs[r], tm, d)
    out_ref[...] = _layer_norm(alpha * h1 + f, g_ref[...], b_ref[...])


def _final(h1, yg, w_tk, wg_bf, wu_bf, wd_bf, g, b, *, tm, alpha, row0, total_rows, out_prev):
    rows, d = h1.shape
    d_exp = wg_bf.shape[1]
    n_slab = d // LANES
    tiles = rows // tm
    first = row0 // tm
    kern = functools.partial(_final_kernel, alpha=alpha)
    y_spec = lambda r: pl.BlockSpec((tm * n_slab, LANES), lambda i: (r * tiles + i, 0))
    in_specs = [pl.BlockSpec((tm, d), lambda i: (i, 0))] + [y_spec(r) for r in range(TOP_K)] + [
        pl.BlockSpec((TOP_K, tm), lambda i: (0, i)),
        pl.BlockSpec((d, d_exp), lambda i: (0, 0)),
        pl.BlockSpec((d, d_exp), lambda i: (0, 0)),
        pl.BlockSpec((d_exp, d), lambda i: (0, 0)),
        pl.BlockSpec((1, d), lambda i: (0, 0)),
        pl.BlockSpec((1, d), lambda i: (0, 0)),
    ]
    args = [h1, *([yg] * TOP_K), w_tk, wg_bf, wu_bf, wd_bf, g, b]
    aliases = {}
    if out_prev is not None:
        in_specs.append(pl.BlockSpec(memory_space=pl.ANY))
        aliases = {len(args): 0}
        args.append(out_prev)
    return pl.pallas_call(
        kern,
        grid=(tiles,),
        in_specs=in_specs,
        out_specs=pl.BlockSpec((tm, d), lambda i: (first + i, 0)),
        out_shape=jax.ShapeDtypeStruct((total_rows, d), F32),
        input_output_aliases=aliases,
        compiler_params=_params("arbitrary"),
        name="final",
    )(*args)


def _rope_tables(pos, dh):
    half = dh // 2
    inv_freq = ROPE_THETA ** (-jnp.arange(half, dtype=F32) / half)
    ang = pos.astype(F32)[:, None] * inv_freq[None, :]
    cos = jnp.cos(ang)
    sin = jnp.sin(ang)
    reps = LANES // dh
    cos_full = jnp.tile(jnp.concatenate([cos, cos], axis=1), (1, reps))
    sin_full = jnp.tile(jnp.concatenate([-sin, sin], axis=1), (1, reps))
    return cos_full, sin_full


def _tile(n, pref):
    t = min(n, pref)
    assert n % t == 0, (n, pref)
    return t


def kernel(x, meta_tokens, ln0_g, ln0_b, w_in, lambda_q1, lambda_k1, lambda_q2, lambda_k2, subln_g, w_conv, w_proj_attn, w_proj_conv, w_out, ln1_g, ln1_b, w_router, router_bias, w_exp_gate, w_exp_up, w_exp_down, w_sh_gate, w_sh_up, w_sh_down, ln2_g, ln2_b):
    batch, seq, d = x.shape
    n_meta = meta_tokens.shape[0]
    depth = w_in.shape[0]
    dh = lambda_q1.shape[-1]
    n_exp = w_router.shape[-1]
    assert depth == 1 and 2 * dh == LANES and n_meta == BF16_SUBLANES
    assert n_exp // N_GROUPS == 8 and subln_g.shape[-1] == 2 * dh
    cd = d // 2
    heads = cd // (2 * dh)
    n_tok = batch * seq
    alpha = float((2 * depth) ** 0.25)

    row = lambda a: a.reshape(1, -1).astype(F32)

    w_in_bf = w_in[0].astype(BF16)
    cos_m, sin_m = _rope_tables(jnp.arange(n_meta), dh)
    cos_r, sin_r = _rope_tables(jnp.arange(n_meta, n_meta + seq), dh)
    tn_in = _tile(cd, 1024)
    tm_in = _tile(seq, 512)
    inproj = functools.partial(_in_proj, g=row(ln0_g), b=row(ln0_b), w_bf=w_in_bf, tn=tn_in, qk_width=cd, dh=dh)
    u, h = inproj(x.reshape(n_tok, d), cos=cos_r, sin=sin_r, tm=tm_in)
    u_meta, _ = inproj(meta_tokens.astype(F32), cos=cos_m, sin=sin_m, tm=n_meta)

    lamv = jnp.stack([lambda_q1[0], lambda_k1[0], lambda_q2[0], lambda_k2[0]]).astype(F32)
    o_n = _attention(u, u_meta, lamv, row(subln_g[0]), batch=batch, seq=seq, heads=heads, dh=dh,
                     tq=_tile(seq, 512))

    merged = _merge(o_n, u, u_meta, w_conv[0].astype(F32), w_proj_attn[0].astype(BF16),
                    w_proj_conv[0].astype(BF16), seq=seq, d=d, tm=_tile(seq, 512), tn=_tile(d, 1024))

    wo_bf = w_out[0].astype(BF16)
    w_router_t = w_router[0].T.astype(F32)
    router_bias_col = router_bias[0].reshape(n_exp, 1).astype(F32)
    shared_bf = (w_sh_gate[0].astype(BF16), w_sh_up[0].astype(BF16), w_sh_down[0].astype(BF16))

    def moe_group(row0, n_tok, out_prev):
        h1, h1s, eidx, pos_tk, w_tk, counts = _out_router(
            merged, h, wo_bf, row(ln1_g[0]), row(ln1_b[0]), w_router_t, router_bias_col,
            tm=_tile(seq, 256), alpha=alpha, row0=row0, rows=n_tok)

        bm = 256
        n_slab = d // LANES
        n_asg = n_tok * TOP_K
        nb = -(-n_asg // bm) + n_exp
        counts = counts[:, 0].astype(jnp.int32)
        ends = jnp.cumsum(counts)
        starts = ends - counts
        padded = (counts + bm - 1) // bm * bm
        pad_end = jnp.cumsum(padded)
        pad_start = pad_end - padded
        experts = jnp.arange(n_exp, dtype=jnp.int32)
        onehot = eidx[..., None] == experts
        lookup = lambda table: jnp.sum(jnp.where(onehot, table, 0), axis=-1)
        dense_tk = pos_tk + lookup(starts)
        dest_tk = pos_tk + lookup(pad_start)
        tok = jnp.broadcast_to(jnp.arange(n_tok, dtype=jnp.int32)[None], dense_tk.shape)
        _, sorted_tok = lax.sort((dense_tk.reshape(-1), tok.reshape(-1)), num_keys=1)
        block_row = jnp.arange(nb, dtype=jnp.int32) * bm
        block_e = jnp.minimum(jnp.sum(pad_end[None, :] <= block_row[:, None], axis=1), n_exp - 1).astype(jnp.int32)
        n_valid = (pad_end[-1:] // bm).astype(jnp.int32)
        block_hot = block_e[:, None] == experts
        block_lookup = lambda table: jnp.sum(jnp.where(block_hot, table, 0), axis=-1)
        in_expert = (block_row - block_lookup(pad_start))[:, None] + jnp.arange(bm, dtype=jnp.int32)
        dense_row = jnp.minimum(block_lookup(starts)[:, None] + in_expert, n_asg - 1)
        spread = (block_row[:, None] + jnp.arange(bm, dtype=jnp.int32)) % n_tok
        row_tok = jnp.where(in_expert < block_lookup(counts)[:, None],
                            sorted_tok.at[dense_row].get(mode="promise_in_bounds"), spread).reshape(-1)

        xs = h1s.reshape(n_tok, n_slab, LANES).at[row_tok].get(mode="promise_in_bounds")
        ys = _experts(block_e, n_valid, xs.reshape(nb * bm * n_slab, LANES),
                      w_exp_gate[0], w_exp_up[0], w_exp_down[0], bm=bm)
        yg = ys.reshape(nb * bm, n_slab, LANES).at[dest_tk.reshape(-1)].get(mode="promise_in_bounds")

        return _final(h1, yg.reshape(n_asg * n_slab, LANES), w_tk, *shared_bf, row(ln2_g[0]), row(ln2_b[0]),
                      tm=_tile(seq, 128), alpha=alpha, row0=row0, total_rows=batch * seq, out_prev=out_prev)

    n_groups = 2 if batch % 2 == 0 else 1
    group_rows = batch * seq // n_groups
    out = None
    for gi in range(n_groups):
        out = moe_group(gi * group_rows, group_rows, out)
    return out.reshape(batch, seq, d)
```

```python
import functools

import jax
import jax.numpy as jnp
import numpy as np
from jax import lax
from jax.experimental import pallas as pl
from jax.experimental.pallas import tpu as pltpu

CHUNK = 64
ROPE_THETA = 10000.0
LN_EPS = 1e-5
TOP_K = 8
N_GROUPS = 8
TOPK_GROUPS = 4
ROUTED_SCALE = 2.5
LAM_INIT = 0.2
LOG2E = 1.4426950408889634

LANES = 128
BF16_SUBLANES = 16
VMEM_LIMIT_BYTES = 56 * 1024 * 1024

F32 = jnp.float32
BF16 = jnp.bfloat16


def _params(*sem):
    return pltpu.CompilerParams(dimension_semantics=sem, vmem_limit_bytes=VMEM_LIMIT_BYTES)


def _layer_norm(x, g, b):
    mu = jnp.mean(x, axis=-1, keepdims=True)
    xc = x - mu
    var = jnp.mean(xc * xc, axis=-1, keepdims=True)
    return xc * lax.rsqrt(var + LN_EPS) * g + b


def _dot(a, b):
    return jnp.dot(a, b, preferred_element_type=F32)


def _dot_nt(a, b, **kw):
    return lax.dot_general(a, b, (((1,), (1,)), ((), ())), preferred_element_type=F32, **kw)


def _sigmoid(x):
    return 1.0 / (1.0 + jnp.exp(-x))


def _store_slabs(ref, x, stage):
    rows, d = x.shape
    n_slab = d // LANES
    for c in range(n_slab):
        stage[pl.ds(c, rows, stride=n_slab), :] = x[:, c * LANES:(c + 1) * LANES]
    ref[...] = stage[...].astype(ref.dtype)


def _load_slabs(ref, stage, rows, d):
    n_slab = d // LANES
    stage[...] = ref[...].astype(F32)
    return jnp.concatenate([stage[pl.ds(c, rows, stride=n_slab), :] for c in range(n_slab)], axis=1)


def _inproj_kernel(x_ref, g_ref, b_ref, w_ref, cos_ref, sin_ref, u_ref, h_ref, hb_ref,
                   *, q_tiles, qk_scale, half):
    j = pl.program_id(1)

    @pl.when(j == 0)
    def _():
        h = _layer_norm(x_ref[...], g_ref[...], b_ref[...])
        h_ref[...] = h
        hb_ref[...] = h.astype(BF16)

    acc = _dot(hb_ref[...], w_ref[...])
    tn = acc.shape[1]

    @pl.when(j < 2 * q_tiles)
    def _():
        cos = cos_ref[...]
        sin = sin_ref[...]
        lane = lax.broadcasted_iota(jnp.int32, cos.shape, 1)
        first = (lane % (2 * half)) < half
        scale = jnp.where(j < q_tiles, qk_scale, 1.0).astype(F32)
        for c in range(tn // LANES):
            a = acc[:, c * LANES:(c + 1) * LANES]
            partner = jnp.where(first, pltpu.roll(a, LANES - half, axis=1), pltpu.roll(a, half, axis=1))
            u_ref[:, c * LANES:(c + 1) * LANES] = ((a * cos + partner * sin) * scale).astype(BF16)

    @pl.when(j >= 2 * q_tiles)
    def _():
        u_ref[...] = acc.astype(BF16)


def _in_proj(x2d, g, b, w_bf, cos, sin, *, tm, tn, qk_width, dh):
    rows, d = x2d.shape
    cols = w_bf.shape[1]
    pos_blocks = cos.shape[0] // tm
    kern = functools.partial(_inproj_kernel, q_tiles=qk_width // tn, qk_scale=float(dh) ** -0.5 * LOG2E, half=dh // 2)
    return pl.pallas_call(
        kern,
        grid=(rows // tm, cols // tn),
        in_specs=[
            pl.BlockSpec((tm, d), lambda i, j: (i, 0)),
            pl.BlockSpec((1, d), lambda i, j: (0, 0)),
            pl.BlockSpec((1, d), lambda i, j: (0, 0)),
            pl.BlockSpec((d, tn), lambda i, j: (0, j)),
            pl.BlockSpec((tm, LANES), lambda i, j: (i % pos_blocks, 0)),
            pl.BlockSpec((tm, LANES), lambda i, j: (i % pos_blocks, 0)),
        ],
        out_specs=[
            pl.BlockSpec((tm, tn), lambda i, j: (i, j)),
            pl.BlockSpec((tm, d), lambda i, j: (i, 0)),
        ],
        out_shape=[
            jax.ShapeDtypeStruct((rows, cols), BF16),
            jax.ShapeDtypeStruct((rows, d), F32),
        ],
        scratch_shapes=[pltpu.VMEM((tm, d), BF16)],
        compiler_params=_params("arbitrary", "arbitrary"),
        name="in_proj",
    )(x2d, g, b, w_bf, cos, sin)


def _attn_kernel(q_ref, k_ref, v_ref, km_ref, vm_ref, lamv_ref, g_ref, o_ref, vt_ref, vmt_ref, s_ref, *, tq, dh):
    qi = pl.program_id(2)
    seq = k_ref.shape[0]

    @pl.when(qi == 0)
    def _():
        for c in range(seq // tq):
            vt_ref[:, c * tq:(c + 1) * tq] = v_ref[c * tq:(c + 1) * tq, :].astype(F32).T.astype(BF16)
        vmt_ref[...] = vm_ref[...].astype(F32).T.astype(BF16)

    qt = q_ref[...].astype(F32).T
    dim = lax.broadcasted_iota(jnp.int32, qt.shape, 0)
    qq = jnp.concatenate([jnp.where(dim < dh, qt, 0.0), jnp.where(dim >= dh, qt, 0.0)], axis=1).astype(BF16)

    s = _dot(km_ref[...], qq)
    m = jnp.max(s, axis=0, keepdims=True)
    p = jnp.exp2(s - m)
    l = jnp.sum(p, axis=0, keepdims=True)
    acc = _dot(vmt_ref[...], p.astype(BF16))

    def update(carry, s, vt):
        m, l, acc = carry
        m_new = jnp.maximum(m, jnp.max(s, axis=0, keepdims=True))
        a = jnp.exp2(m - m_new)
        p = jnp.exp2(s - m_new)
        l = a * l + jnp.sum(p, axis=0, keepdims=True)
        acc = a * acc + _dot(vt, p.astype(BF16))
        return m_new, l, acc

    def scores(kb):
        return _dot(k_ref[pl.ds(pl.multiple_of(kb * tq, tq), tq), :], qq)

    def values(kb):
        return vt_ref[:, pl.ds(pl.multiple_of(kb * tq, tq), tq)]

    def block_pair(i, carry):
        s_ref[1] = scores(2 * i + 1)
        carry = update(carry, s_ref[0], values(2 * i))
        s_ref[0] = scores(2 * i + 2)
        return update(carry, s_ref[1], values(2 * i + 1))

    def odd_block(carry):
        s_ref[1] = scores(qi)
        carry = update(carry, s_ref[0], values(qi - 1))
        s_ref[0] = s_ref[1]
        return carry

    s_ref[0] = scores(0)
    carry = lax.fori_loop(0, qi // 2, block_pair, (m, l, acc))
    m, l, acc = lax.cond(qi % 2 == 1, odd_block, lambda c: c, carry)
    s = s_ref[0]

    start = pl.multiple_of(qi * tq, tq)
    key = lax.broadcasted_iota(jnp.int32, s.shape, 0)
    qry = lax.broadcasted_iota(jnp.int32, s.shape, 1)
    qry = jnp.where(qry >= tq, qry - tq, qry)
    s = jnp.where((key // CHUNK) <= (qry // CHUNK), s, -jnp.inf)
    m, l, acc = update((m, l, acc), s, vt_ref[:, pl.ds(start, tq)])

    lamv = lamv_ref[...]
    lam = (jnp.exp(jnp.sum(lamv[0:1] * lamv[1:2], axis=1, keepdims=True))
           - jnp.exp(jnp.sum(lamv[2:3] * lamv[3:4], axis=1, keepdims=True)) + LAM_INIT)
    o_all = acc / l
    o = (o_all[:, :tq] - lam * o_all[:, tq:]).T
    ms = jnp.mean(o * o, axis=-1, keepdims=True)
    o = o * lax.rsqrt(ms + LN_EPS) * g_ref[...] * (1.0 - LAM_INIT)
    o_ref[...] = o.astype(BF16)


def _attention(u, u_meta, lamv, subln_g, *, batch, seq, heads, dh, tq):
    vd = 2 * dh
    nq = seq // tq
    kern = functools.partial(_attn_kernel, tq=tq, dh=dh)
    n_meta = u_meta.shape[0]
    return pl.pallas_call(
        kern,
        grid=(batch, heads, nq),
        in_specs=[
            pl.BlockSpec((tq, vd), lambda b, h, i: (b * nq + i, h)),
            pl.BlockSpec((seq, vd), lambda b, h, i: (b, heads + h)),
            pl.BlockSpec((seq, vd), lambda b, h, i: (b, 2 * heads + h)),
            pl.BlockSpec((n_meta, vd), lambda b, h, i: (0, heads + h)),
            pl.BlockSpec((n_meta, vd), lambda b, h, i: (0, 2 * heads + h)),
            pl.BlockSpec((4, dh), lambda b, h, i: (0, 0)),
            pl.BlockSpec((1, vd), lambda b, h, i: (0, 0)),
        ],
        out_specs=pl.BlockSpec((tq, vd), lambda b, h, i: (b * nq + i, h)),
        out_shape=jax.ShapeDtypeStruct((batch * seq, heads * vd), BF16),
        scratch_shapes=[pltpu.VMEM((vd, seq), BF16), pltpu.VMEM((vd, n_meta), BF16),
                        pltpu.VMEM((2, tq, 2 * tq), F32)],
        compiler_params=_params("arbitrary", "arbitrary", "arbitrary"),
        name="attention",
    )(u, u, u, u_meta, u_meta, lamv, subln_g)


def _merge_kernel(o_ref, cx_ref, cc_ref, cb_ref, hx_ref, hc_ref, mx_ref, mc_ref, wconv_ref,
                  ga_ref, gc_ref, wa_ref, wc_ref, out_ref, y_ref, *, tiles_per_seq):
    i = pl.program_id(0)
    j = pl.program_id(1)

    @pl.when(j == 0)
    def _():
        z = cc_ref[...].astype(F32) * cx_ref[...].astype(F32)
        cb = cb_ref[...].astype(F32)
        w = wconv_ref[...]
        w0, w1, w2 = w[0:1], w[1:2], w[2:3]
        y_ref[...] = (cb * (w0 * pltpu.roll(z, 2, axis=0) + w1 * pltpu.roll(z, 1, axis=0) + w2 * z)).astype(BF16)
        hb = BF16_SUBLANES
        first = (i % tiles_per_seq) == 0
        hz_prev = hc_ref[...].astype(F32) * hx_ref[...].astype(F32)
        hz_meta = mc_ref[...].astype(F32) * mx_ref[...].astype(F32)
        hz = jnp.where(first, hz_meta, hz_prev)
        zm1 = hz[hb - 1:hb]
        zm2 = hz[hb - 2:hb - 1]
        zh = z[0:hb]
        row = lax.broadcasted_iota(jnp.int32, zh.shape, 0)
        z1 = jnp.where(row == 0, zm1, pltpu.roll(zh, 1, axis=0))
        z2 = jnp.where(row == 0, zm2, jnp.where(row == 1, zm1, pltpu.roll(zh, 2, axis=0)))
        y_ref[0:hb, :] = (cb[0:hb] * (w0 * z2 + w1 * z1 + w2 * zh)).astype(BF16)

    pa = _dot(o_ref[...], wa_ref[...])
    pc = _dot(y_ref[...], wc_ref[...])
    out = _sigmoid(ga_ref[...].astype(F32)) * pa + _sigmoid(gc_ref[...].astype(F32)) * pc
    out_ref[...] = out.astype(BF16)


def _merge(o_n, u, u_meta, w_conv, wa_bf, wc_bf, *, seq, d, tm, tn):
    rows = o_n.shape[0]
    cd = d // 2
    hb = BF16_SUBLANES
    kern = functools.partial(_merge_kernel, tiles_per_seq=seq // tm)
    halo = lambda c: pl.BlockSpec((hb, cd), lambda i, j: (jnp.maximum(i * (tm // hb) - 1, 0), c))
    return pl.pallas_call(
        kern,
        grid=(rows // tm, d // tn),
        in_specs=[
            pl.BlockSpec((tm, cd), lambda i, j: (i, 0)),
            pl.BlockSpec((tm, cd), lambda i, j: (i, 3)),
            pl.BlockSpec((tm, cd), lambda i, j: (i, 4)),
            pl.BlockSpec((tm, cd), lambda i, j: (i, 5)),
            halo(3),
            halo(4),
            pl.BlockSpec((hb, cd), lambda i, j: (0, 3)),
            pl.BlockSpec((hb, cd), lambda i, j: (0, 4)),
            pl.BlockSpec((3, cd), lambda i, j: (0, 0)),
            pl.BlockSpec((tm, tn), lambda i, j: (i, 3 * d // tn + j)),
            pl.BlockSpec((tm, tn), lambda i, j: (i, 4 * d // tn + j)),
            pl.BlockSpec((cd, tn), lambda i, j: (0, j)),
            pl.BlockSpec((cd, tn), lambda i, j: (0, j)),
        ],
        out_specs=pl.BlockSpec((tm, tn), lambda i, j: (i, j)),
        out_shape=jax.ShapeDtypeStruct((rows, d), BF16),
        scratch_shapes=[pltpu.VMEM((tm, cd), BF16)],
        compiler_params=_params("arbitrary", "arbitrary"),
        name="merge",
    )(o_n, u, u, u, u, u, u_meta, u_meta, w_conv, u, u, wa_bf, wc_bf)


def _out_router_kernel(mg_ref, h_ref, wo_ref, g_ref, b_ref, wr_ref, rb_ref, tri_ref,
                       h1_ref, h1s_ref, eidx_ref, pos_ref, wtk_ref, counts_ref, cnt_ref, stage_ref, *, alpha):
    @pl.when(pl.program_id(0) == 0)
    def _():
        cnt_ref[...] = jnp.zeros_like(cnt_ref)

    m = _dot(mg_ref[...], wo_ref[...])
    h1 = _layer_norm(alpha * h_ref[...] + m, g_ref[...], b_ref[...])
    h1_ref[...] = h1
    _store_slabs(h1s_ref, h1, stage_ref)

    logits = _dot_nt(wr_ref[...], h1, precision=lax.Precision.HIGHEST)
    s = _sigmoid(logits)
    sel = s + rb_ref[...]
    n_exp, tm = sel.shape
    per_group = n_exp // N_GROUPS
    neg = -jnp.inf

    grow = lax.broadcasted_iota(jnp.int32, (per_group, tm), 0).astype(F32)
    scores = []
    for g in range(N_GROUPS):
        sg = sel[g * per_group:(g + 1) * per_group]
        m1 = jnp.max(sg, axis=0, keepdims=True)
        first = jnp.min(jnp.where(sg == m1, grow, float(per_group)), axis=0, keepdims=True)
        m2 = jnp.max(jnp.where(grow == first, neg, sg), axis=0, keepdims=True)
        scores.append(m1 + m2)
    gs = jnp.concatenate(scores, axis=0)

    gidx = lax.broadcasted_iota(jnp.int32, gs.shape, 0)
    grank = jnp.zeros(gs.shape, F32)
    for g in range(N_GROUPS):
        o = gs[g:g + 1]
        grank = grank + jnp.where((o > gs) | ((o == gs) & (gidx > g)), 1.0, 0.0)
    gkeep = jnp.where(grank < TOPK_GROUPS, 1.0, 0.0)
    keep = jnp.concatenate([jnp.broadcast_to(gkeep[g:g + 1], (per_group, tm)) for g in range(N_GROUPS)], axis=0)
    selm = jnp.where(keep > 0.5, sel, neg)

    eidx = lax.broadcasted_iota(jnp.int32, selm.shape, 0)
    rank = jnp.zeros(selm.shape, F32)
    for e in range(n_exp):
        o = selm[e:e + 1]
        rank = rank + jnp.where((o > selm) | ((o == selm) & (eidx > e)), 1.0, 0.0)
    chosen = rank < TOP_K
    ssel = jnp.where(chosen, s, 0.0)
    w = ssel / jnp.sum(ssel, axis=0, keepdims=True) * ROUTED_SCALE

    chosen_f = jnp.where(chosen, 1.0, 0.0)
    before = cnt_ref[...]
    pos = _dot(chosen_f.astype(BF16), tri_ref[...]) + before
    cnt_ref[...] = before + jnp.sum(chosen_f, axis=1, keepdims=True)
    counts_ref[...] = cnt_ref[...]

    def pick(v):
        rows = [jnp.sum(jnp.where(rank == float(r), v, 0.0), axis=0, keepdims=True) for r in range(TOP_K)]
        return jnp.concatenate(rows, axis=0)

    eidx_ref[...] = pick(eidx.astype(F32)).astype(jnp.int32)
    pos_ref[...] = pick(pos).astype(jnp.int32)
    wtk_ref[...] = pick(w)


def _out_router(merged, h, wo_bf, g, b, w_router_t, router_bias, *, tm, alpha, row0, rows):
    d = merged.shape[1]
    first = row0 // tm
    n_exp = w_router_t.shape[0]
    n_slab = d // LANES
    tri = jnp.triu(jnp.ones((tm, tm), BF16), 1)
    kern = functools.partial(_out_router_kernel, alpha=alpha)
    per_tok = lambda dt: jax.ShapeDtypeStruct((TOP_K, rows), dt)
    return pl.pallas_call(
        kern,
        grid=(rows // tm,),
        in_specs=[
            pl.BlockSpec((tm, d), lambda i: (first + i, 0)),
            pl.BlockSpec((tm, d), lambda i: (first + i, 0)),
            pl.BlockSpec((d, d), lambda i: (0, 0)),
            pl.BlockSpec((1, d), lambda i: (0, 0)),
            pl.BlockSpec((1, d), lambda i: (0, 0)),
            pl.BlockSpec((n_exp, d), lambda i: (0, 0)),
            pl.BlockSpec((n_exp, 1), lambda i: (0, 0)),
            pl.BlockSpec((tm, tm), lambda i: (0, 0)),
        ],
        out_specs=[
            pl.BlockSpec((tm, d), lambda i: (i, 0)),
            pl.BlockSpec((tm * n_slab, LANES), lambda i: (i, 0)),
            pl.BlockSpec((TOP_K, tm), lambda i: (0, i)),
            pl.BlockSpec((TOP_K, tm), lambda i: (0, i)),
            pl.BlockSpec((TOP_K, tm), lambda i: (0, i)),
            pl.BlockSpec((n_exp, 1), lambda i: (0, 0)),
        ],
        out_shape=[
            jax.ShapeDtypeStruct((rows, d), F32),
            jax.ShapeDtypeStruct((rows * n_slab, LANES), BF16),
            per_tok(jnp.int32),
            per_tok(jnp.int32),
            per_tok(F32),
            jax.ShapeDtypeStruct((n_exp, 1), F32),
        ],
        scratch_shapes=[pltpu.VMEM((n_exp, 1), F32), pltpu.VMEM((tm * n_slab, LANES), F32)],
        compiler_params=_params("arbitrary"),
        name="out_router",
    )(merged, h, wo_bf, g, b, w_router_t, router_bias, tri)


def _expert_kernel(be_ref, nv_ref, xs_ref, wg_ref, wu_ref, wd_ref, ys_ref, wg_bf, wu_bf, wd_bf, slot_ref,
                   stage_in, stage_out):
    s = pl.program_id(0)
    last = pl.num_programs(0) - 2
    cur = be_ref[jnp.maximum(s - 1, 0)]
    nxt = be_ref[jnp.minimum(s, last)]

    @pl.when(s == 0)
    def _():
        slot_ref[0] = 0

    slot = slot_ref[0]
    fill = jnp.where(s == 0, 0, 1 - slot)

    @pl.when((s == 0) | (nxt != cur))
    def _():
        wg_bf[fill] = wg_ref[...].astype(BF16)
        wu_bf[fill] = wu_ref[...].astype(BF16)
        wd_bf[fill] = wd_ref[...].astype(BF16)

    block = s - 1

    @pl.when((block >= 0) & (block < nv_ref[0]))
    def _():
        d, d_exp = wg_ref.shape
        x = _load_slabs(xs_ref, stage_in, xs_ref.shape[0] * LANES // d, d).astype(BF16)
        g = _dot(x, wg_bf[slot])
        u = _dot(x, wu_bf[slot])
        hid = g * _sigmoid(g) * u
        _store_slabs(ys_ref, _dot(hid.astype(BF16), wd_bf[slot]), stage_out)

    @pl.when(block >= nv_ref[0])
    def _():
        ys_ref[...] = jnp.zeros_like(ys_ref)

    @pl.when((s > 0) & (nxt != cur))
    def _():
        slot_ref[0] = 1 - slot


def _experts(block_e, n_valid, xs, w_gate, w_up, w_down, *, bm):
    _, d, d_exp = w_gate.shape
    n_slab = d // LANES
    nb = xs.shape[0] // (bm * n_slab)
    ahead = lambda s, be, nv: (be[jnp.minimum(s, nb - 1)], 0, 0)
    grid_spec = pltpu.PrefetchScalarGridSpec(
        num_scalar_prefetch=2,
        grid=(nb + 1,),
        in_specs=[
            pl.BlockSpec((bm * n_slab, LANES), lambda s, be, nv: (jnp.maximum(jnp.minimum(s - 1, nv[0] - 1), 0), 0)),
            pl.BlockSpec((None, d, d_exp), ahead),
            pl.BlockSpec((None, d, d_exp), ahead),
            pl.BlockSpec((None, d_exp, d), ahead),
        ],
        out_specs=pl.BlockSpec((bm * n_slab, LANES), lambda s, be, nv: (jnp.maximum(s - 1, 0), 0)),
        scratch_shapes=[
            pltpu.VMEM((2, d, d_exp), BF16),
            pltpu.VMEM((2, d, d_exp), BF16),
            pltpu.VMEM((2, d_exp, d), BF16),
            pltpu.SMEM((1,), jnp.int32),
            pltpu.VMEM((bm * n_slab, LANES), F32),
            pltpu.VMEM((bm * n_slab, LANES), F32),
        ],
    )
    return pl.pallas_call(
        _expert_kernel,
        grid_spec=grid_spec,
        out_shape=jax.ShapeDtypeStruct(xs.shape, xs.dtype),
        compiler_params=_params("arbitrary"),
        name="experts",
    )(block_e, n_valid, xs, w_gate, w_up, w_down)


def _final_kernel(*refs, alpha):
    h1_ref = refs[0]
    y_refs = refs[1:1 + TOP_K]
    w_ref, wg_ref, wu_ref, wd_ref, g_ref, b_ref = refs[1 + TOP_K:7 + TOP_K]
    out_ref, stage_a, stage_b = refs[-3:]
    h1 = h1_ref[...]
    tm, d = h1.shape
    x = h1.astype(BF16)
    gate = _dot(x, wg_ref[...])
    up = _dot(x, wu_ref[...])
    hid = gate * _sigmoid(gate) * up
    f = _dot(hid.astype(BF16), wd_ref[...])
    wt = w_ref[...].T
    for r in range(TOP_K):
        f = f + wt[:, r:r + 1] * _load_slabs(y_refs[r], stage_b if r % 2 else stage_a, tm, d)
    out_ref[...] = _layer_norm(alpha * h1 + f, g_ref[...], b_ref[...])


def _final(h1, yg, w_tk, wg_bf, wu_bf, wd_bf, g, b, *, tm, alpha, row0, total_rows, out_prev):
    rows, d = h1.shape
    d_exp = wg_bf.shape[1]
    n_slab = d // LANES
    tiles = rows // tm
    first = row0 // tm
    kern = functools.partial(_final_kernel, alpha=alpha)
    y_spec = lambda r: pl.BlockSpec((tm * n_slab, LANES), lambda i: (r * tiles + i, 0))
    in_specs = [pl.BlockSpec((tm, d), lambda i: (i, 0))] + [y_spec(r) for r in range(TOP_K)] + [
        pl.BlockSpec((TOP_K, tm), lambda i: (0, i)),
        pl.BlockSpec((d, d_exp), lambda i: (0, 0)),
        pl.BlockSpec((d, d_exp), lambda i: (0, 0)),
        pl.BlockSpec((d_exp, d), lambda i: (0, 0)),
        pl.BlockSpec((1, d), lambda i: (0, 0)),
        pl.BlockSpec((1, d), lambda i: (0, 0)),
    ]
    args = [h1, *([yg] * TOP_K), w_tk, wg_bf, wu_bf, wd_bf, g, b]
    aliases = {}
    if out_prev is not None:
        in_specs.append(pl.BlockSpec(memory_space=pl.ANY))
        aliases = {len(args): 0}
        args.append(out_prev)
    return pl.pallas_call(
        kern,
        grid=(tiles,),
        in_specs=in_specs,
        out_specs=pl.BlockSpec((tm, d), lambda i: (first + i, 0)),
        out_shape=jax.ShapeDtypeStruct((total_rows, d), F32),
        input_output_aliases=aliases,
        scratch_shapes=[pltpu.VMEM((tm * n_slab, LANES), F32)] * 2,
        compiler_params=_params("arbitrary"),
        name="final",
    )(*args)


def _rope_tables(pos, dh):
    half = dh // 2
    inv_freq = ROPE_THETA ** (-jnp.arange(half, dtype=F32) / half)
    ang = pos.astype(F32)[:, None] * inv_freq[None, :]
    cos = jnp.cos(ang)
    sin = jnp.sin(ang)
    reps = LANES // dh
    cos_full = jnp.tile(jnp.concatenate([cos, cos], axis=1), (1, reps))
    sin_full = jnp.tile(jnp.concatenate([-sin, sin], axis=1), (1, reps))
    return cos_full, sin_full


def _tile(n, pref):
    t = min(n, pref)
    assert n % t == 0, (n, pref)
    return t


def kernel(x, meta_tokens, ln0_g, ln0_b, w_in, lambda_q1, lambda_k1, lambda_q2, lambda_k2, subln_g, w_conv, w_proj_attn, w_proj_conv, w_out, ln1_g, ln1_b, w_router, router_bias, w_exp_gate, w_exp_up, w_exp_down, w_sh_gate, w_sh_up, w_sh_down, ln2_g, ln2_b):
    batch, seq, d = x.shape
    n_meta = meta_tokens.shape[0]
    depth = w_in.shape[0]
    dh = lambda_q1.shape[-1]
    n_exp = w_router.shape[-1]
    assert depth == 1 and 2 * dh == LANES and n_meta == BF16_SUBLANES
    assert n_exp // N_GROUPS == 8 and subln_g.shape[-1] == 2 * dh
    cd = d // 2
    heads = cd // (2 * dh)
    n_tok = batch * seq
    alpha = float((2 * depth) ** 0.25)

    row = lambda a: a.reshape(1, -1).astype(F32)

    w_in_bf = w_in[0].astype(BF16)
    cos_m, sin_m = _rope_tables(jnp.arange(n_meta), dh)
    cos_r, sin_r = _rope_tables(jnp.arange(n_meta, n_meta + seq), dh)
    tn_in = _tile(cd, 1024)
    tm_in = _tile(seq, 512)
    inproj = functools.partial(_in_proj, g=row(ln0_g), b=row(ln0_b), w_bf=w_in_bf, tn=tn_in, qk_width=cd, dh=dh)
    u, h = inproj(x.reshape(n_tok, d), cos=cos_r, sin=sin_r, tm=tm_in)
    u_meta, _ = inproj(meta_tokens.astype(F32), cos=cos_m, sin=sin_m, tm=n_meta)

    lamv = jnp.stack([lambda_q1[0], lambda_k1[0], lambda_q2[0], lambda_k2[0]]).astype(F32)
    o_n = _attention(u, u_meta, lamv, row(subln_g[0]), batch=batch, seq=seq, heads=heads, dh=dh,
                     tq=_tile(seq, 512))

    merged = _merge(o_n, u, u_meta, w_conv[0].astype(F32), w_proj_attn[0].astype(BF16),
                    w_proj_conv[0].astype(BF16), seq=seq, d=d, tm=_tile(seq, 512), tn=_tile(d, 1024))

    wo_bf = w_out[0].astype(BF16)
    w_router_t = w_router[0].T.astype(F32)
    router_bias_col = router_bias[0].reshape(n_exp, 1).astype(F32)
    shared_bf = (w_sh_gate[0].astype(BF16), w_sh_up[0].astype(BF16), w_sh_down[0].astype(BF16))

    def moe_group(row0, n_tok, out_prev):
        h1, h1s, eidx, pos_tk, w_tk, counts = _out_router(
            merged, h, wo_bf, row(ln1_g[0]), row(ln1_b[0]), w_router_t, router_bias_col,
            tm=_tile(seq, 256), alpha=alpha, row0=row0, rows=n_tok)

        bm = 256
        n_slab = d // LANES
        n_asg = n_tok * TOP_K
        nb = -(-n_asg // bm) + n_exp
        counts = counts[:, 0].astype(jnp.int32)
        ends = jnp.cumsum(counts)
        starts = ends - counts
        padded = (counts + bm - 1) // bm * bm
        pad_end = jnp.cumsum(padded)
        pad_start = pad_end - padded
        experts = jnp.arange(n_exp, dtype=jnp.int32)
        onehot = eidx[..., None] == experts
        lookup = lambda table: jnp.sum(jnp.where(onehot, table, 0), axis=-1)
        dense_tk = pos_tk + lookup(starts)
        dest_tk = pos_tk + lookup(pad_start)
        tok = jnp.broadcast_to(jnp.arange(n_tok, dtype=jnp.int32)[None], dense_tk.shape)
        _, sorted_tok = lax.sort((dense_tk.reshape(-1), tok.reshape(-1)), num_keys=1)
        block_row = jnp.arange(nb, dtype=jnp.int32) * bm
        block_e = jnp.minimum(jnp.sum(pad_end[None, :] <= block_row[:, None], axis=1), n_exp - 1).astype(jnp.int32)
        n_valid = (pad_end[-1:] // bm).astype(jnp.int32)
        block_hot = block_e[:, None] == experts
        block_lookup = lambda table: jnp.sum(jnp.where(block_hot, table, 0), axis=-1)
        in_expert = (block_row - block_lookup(pad_start))[:, None] + jnp.arange(bm, dtype=jnp.int32)
        dense_row = jnp.minimum(block_lookup(starts)[:, None] + in_expert, n_asg - 1)
        spread = (block_row[:, None] + jnp.arange(bm, dtype=jnp.int32)) % n_tok
        row_tok = jnp.where(in_expert < block_lookup(counts)[:, None],
                            sorted_tok.at[dense_row].get(mode="promise_in_bounds"), spread).reshape(-1)

        xs = h1s.reshape(n_tok, n_slab, LANES).at[row_tok].get(mode="promise_in_bounds")
        ys = _experts(block_e, n_valid, xs.reshape(nb * bm * n_slab, LANES),
                      w_exp_gate[0], w_exp_up[0], w_exp_down[0], bm=bm)
        yg = ys.reshape(nb * bm, n_slab, LANES).at[dest_tk.reshape(-1)].get(mode="promise_in_bounds")

        return _final(h1, yg.reshape(n_asg * n_slab, LANES), w_tk, *shared_bf, row(ln2_g[0]), row(ln2_b[0]),
                      tm=_tile(seq, 256), alpha=alpha, row0=row0, total_rows=batch * seq, out_prev=out_prev)

    n_groups = 2 if batch % 2 == 0 else 1
    group_rows = batch * seq // n_groups
    out = None
    for gi in range(n_groups):
        out = moe_group(gi * group_rows, group_rows, out)
    return out.reshape(batch, seq, d)
```

```python
import functools

import jax
import jax.numpy as jnp
import numpy as np
from jax import lax
from jax.experimental import pallas as pl
from jax.experimental.pallas import tpu as pltpu

CHUNK = 64
ROPE_THETA = 10000.0
LN_EPS = 1e-5
TOP_K = 8
N_GROUPS = 8
TOPK_GROUPS = 4
ROUTED_SCALE = 2.5
LAM_INIT = 0.2
LOG2E = 1.4426950408889634

LANES = 128
BF16_SUBLANES = 16
VMEM_LIMIT_BYTES = 56 * 1024 * 1024

F32 = jnp.float32
BF16 = jnp.bfloat16


def _params(*sem):
    return pltpu.CompilerParams(dimension_semantics=sem, vmem_limit_bytes=VMEM_LIMIT_BYTES)


def _layer_norm(x, g, b):
    mu = jnp.mean(x, axis=-1, keepdims=True)
    xc = x - mu
    var = jnp.mean(xc * xc, axis=-1, keepdims=True)
    return xc * lax.rsqrt(var + LN_EPS) * g + b


def _dot(a, b):
    return jnp.dot(a, b, preferred_element_type=F32)


def _dot_nt(a, b, **kw):
    return lax.dot_general(a, b, (((1,), (1,)), ((), ())), preferred_element_type=F32, **kw)


def _sigmoid(x):
    return 1.0 / (1.0 + jnp.exp(-x))


def _store_slabs(ref, x, stage):
    rows, d = x.shape
    n_slab = d // LANES
    for c in range(n_slab):
        stage[pl.ds(c, rows, stride=n_slab), :] = x[:, c * LANES:(c + 1) * LANES]
    ref[...] = stage[...].astype(ref.dtype)


def _load_slabs(ref, stage, rows, d):
    n_slab = d // LANES
    stage[...] = ref[...].astype(F32)
    return jnp.concatenate([stage[pl.ds(c, rows, stride=n_slab), :] for c in range(n_slab)], axis=1)


def _inproj_kernel(x_ref, g_ref, b_ref, w_ref, cos_ref, sin_ref, u_ref, h_ref, hb_ref,
                   *, q_tiles, qk_scale, half):
    j = pl.program_id(1)

    @pl.when(j == 0)
    def _():
        h = _layer_norm(x_ref[...], g_ref[...], b_ref[...])
        h_ref[...] = h
        hb_ref[...] = h.astype(BF16)

    acc = _dot(hb_ref[...], w_ref[...])
    tn = acc.shape[1]

    @pl.when(j < 2 * q_tiles)
    def _():
        cos = cos_ref[...]
        sin = sin_ref[...]
        lane = lax.broadcasted_iota(jnp.int32, cos.shape, 1)
        first = (lane % (2 * half)) < half
        scale = jnp.where(j < q_tiles, qk_scale, 1.0).astype(F32)
        for c in range(tn // LANES):
            a = acc[:, c * LANES:(c + 1) * LANES]
            partner = jnp.where(first, pltpu.roll(a, LANES - half, axis=1), pltpu.roll(a, half, axis=1))
            u_ref[:, c * LANES:(c + 1) * LANES] = ((a * cos + partner * sin) * scale).astype(BF16)

    @pl.when(j >= 2 * q_tiles)
    def _():
        u_ref[...] = acc.astype(BF16)


def _in_proj(x2d, g, b, w_bf, cos, sin, *, tm, tn, qk_width, dh):
    rows, d = x2d.shape
    cols = w_bf.shape[1]
    pos_blocks = cos.shape[0] // tm
    kern = functools.partial(_inproj_kernel, q_tiles=qk_width // tn, qk_scale=float(dh) ** -0.5 * LOG2E, half=dh // 2)
    return pl.pallas_call(
        kern,
        grid=(rows // tm, cols // tn),
        in_specs=[
            pl.BlockSpec((tm, d), lambda i, j: (i, 0)),
            pl.BlockSpec((1, d), lambda i, j: (0, 0)),
            pl.BlockSpec((1, d), lambda i, j: (0, 0)),
            pl.BlockSpec((d, tn), lambda i, j: (0, j)),
            pl.BlockSpec((tm, LANES), lambda i, j: (i % pos_blocks, 0)),
            pl.BlockSpec((tm, LANES), lambda i, j: (i % pos_blocks, 0)),
        ],
        out_specs=[
            pl.BlockSpec((tm, tn), lambda i, j: (i, j)),
            pl.BlockSpec((tm, d), lambda i, j: (i, 0)),
        ],
        out_shape=[
            jax.ShapeDtypeStruct((rows, cols), BF16),
            jax.ShapeDtypeStruct((rows, d), F32),
        ],
        scratch_shapes=[pltpu.VMEM((tm, d), BF16)],
        compiler_params=_params("arbitrary", "arbitrary"),
        name="in_proj",
    )(x2d, g, b, w_bf, cos, sin)


def _attn_kernel(q_ref, k_ref, v_ref, km_ref, vm_ref, lamv_ref, g_ref, o_ref, vt_ref, vmt_ref, s_ref, *, tq, dh):
    qi = pl.program_id(2)
    seq = k_ref.shape[0]

    @pl.when(qi == 0)
    def _():
        for c in range(seq // tq):
            vt_ref[:, c * tq:(c + 1) * tq] = v_ref[c * tq:(c + 1) * tq, :].astype(F32).T.astype(BF16)
        vmt_ref[...] = vm_ref[...].astype(F32).T.astype(BF16)

    qt = q_ref[...].astype(F32).T
    dim = lax.broadcasted_iota(jnp.int32, qt.shape, 0)
    qq = jnp.concatenate([jnp.where(dim < dh, qt, 0.0), jnp.where(dim >= dh, qt, 0.0)], axis=1).astype(BF16)

    s = _dot(km_ref[...], qq)
    m = jnp.max(s, axis=0, keepdims=True)
    p = jnp.exp2(s - m)
    l = jnp.sum(p, axis=0, keepdims=True)
    acc = _dot(vmt_ref[...], p.astype(BF16))

    def update(carry, s, vt):
        m, l, acc = carry
        m_new = jnp.maximum(m, jnp.max(s, axis=0, keepdims=True))
        a = jnp.exp2(m - m_new)
        p = jnp.exp2(s - m_new)
        l = a * l + jnp.sum(p, axis=0, keepdims=True)
        acc = a * acc + _dot(vt, p.astype(BF16))
        return m_new, l, acc

    def scores(kb):
        return _dot(k_ref[pl.ds(pl.multiple_of(kb * tq, tq), tq), :], qq)

    def values(kb):
        return vt_ref[:, pl.ds(pl.multiple_of(kb * tq, tq), tq)]

    def block_pair(i, carry):
        s_ref[1] = scores(2 * i + 1)
        carry = update(carry, s_ref[0], values(2 * i))
        s_ref[0] = scores(2 * i + 2)
        return update(carry, s_ref[1], values(2 * i + 1))

    def odd_block(carry):
        s_ref[1] = scores(qi)
        carry = update(carry, s_ref[0], values(qi - 1))
        s_ref[0] = s_ref[1]
        return carry

    s_ref[0] = scores(0)
    carry = lax.fori_loop(0, qi // 2, block_pair, (m, l, acc))
    m, l, acc = lax.cond(qi % 2 == 1, odd_block, lambda c: c, carry)
    s = s_ref[0]

    start = pl.multiple_of(qi * tq, tq)
    key = lax.broadcasted_iota(jnp.int32, s.shape, 0)
    qry = lax.broadcasted_iota(jnp.int32, s.shape, 1)
    qry = jnp.where(qry >= tq, qry - tq, qry)
    s = jnp.where((key // CHUNK) <= (qry // CHUNK), s, -jnp.inf)
    m, l, acc = update((m, l, acc), s, vt_ref[:, pl.ds(start, tq)])

    lamv = lamv_ref[...]
    lam = (jnp.exp(jnp.sum(lamv[0:1] * lamv[1:2], axis=1, keepdims=True))
           - jnp.exp(jnp.sum(lamv[2:3] * lamv[3:4], axis=1, keepdims=True)) + LAM_INIT)
    o_all = acc / l
    o = (o_all[:, :tq] - lam * o_all[:, tq:]).T
    ms = jnp.mean(o * o, axis=-1, keepdims=True)
    o = o * lax.rsqrt(ms + LN_EPS) * g_ref[...] * (1.0 - LAM_INIT)
    o_ref[...] = o.astype(BF16)


def _attention(u, u_meta, lamv, subln_g, *, batch, seq, heads, dh, tq):
    vd = 2 * dh
    nq = seq // tq
    kern = functools.partial(_attn_kernel, tq=tq, dh=dh)
    n_meta = u_meta.shape[0]
    return pl.pallas_call(
        kern,
        grid=(batch, heads, nq),
        in_specs=[
            pl.BlockSpec((tq, vd), lambda b, h, i: (b * nq + i, h)),
            pl.BlockSpec((seq, vd), lambda b, h, i: (b, heads + h)),
            pl.BlockSpec((seq, vd), lambda b, h, i: (b, 2 * heads + h)),
            pl.BlockSpec((n_meta, vd), lambda b, h, i: (0, heads + h)),
            pl.BlockSpec((n_meta, vd), lambda b, h, i: (0, 2 * heads + h)),
            pl.BlockSpec((4, dh), lambda b, h, i: (0, 0)),
            pl.BlockSpec((1, vd), lambda b, h, i: (0, 0)),
        ],
        out_specs=pl.BlockSpec((tq, vd), lambda b, h, i: (b * nq + i, h)),
        out_shape=jax.ShapeDtypeStruct((batch * seq, heads * vd), BF16),
        scratch_shapes=[pltpu.VMEM((vd, seq), BF16), pltpu.VMEM((vd, n_meta), BF16),
                        pltpu.VMEM((2, tq, 2 * tq), F32)],
        compiler_params=_params("arbitrary", "arbitrary", "arbitrary"),
        name="attention",
    )(u, u, u, u_meta, u_meta, lamv, subln_g)


def _merge_kernel(o_ref, cx_ref, cc_ref, cb_ref, hx_ref, hc_ref, mx_ref, mc_ref, wconv_ref,
                  ga_ref, gc_ref, wa_ref, wc_ref, out_ref, y_ref, *, tiles_per_seq):
    i = pl.program_id(0)
    j = pl.program_id(1)

    @pl.when(j == 0)
    def _():
        z = cc_ref[...].astype(F32) * cx_ref[...].astype(F32)
        cb = cb_ref[...].astype(F32)
        w = wconv_ref[...]
        w0, w1, w2 = w[0:1], w[1:2], w[2:3]
        y_ref[...] = (cb * (w0 * pltpu.roll(z, 2, axis=0) + w1 * pltpu.roll(z, 1, axis=0) + w2 * z)).astype(BF16)
        hb = BF16_SUBLANES
        first = (i % tiles_per_seq) == 0
        hz_prev = hc_ref[...].astype(F32) * hx_ref[...].astype(F32)
        hz_meta = mc_ref[...].astype(F32) * mx_ref[...].astype(F32)
        hz = jnp.where(first, hz_meta, hz_prev)
        zm1 = hz[hb - 1:hb]
        zm2 = hz[hb - 2:hb - 1]
        zh = z[0:hb]
        row = lax.broadcasted_iota(jnp.int32, zh.shape, 0)
        z1 = jnp.where(row == 0, zm1, pltpu.roll(zh, 1, axis=0))
        z2 = jnp.where(row == 0, zm2, jnp.where(row == 1, zm1, pltpu.roll(zh, 2, axis=0)))
        y_ref[0:hb, :] = (cb[0:hb] * (w0 * z2 + w1 * z1 + w2 * zh)).astype(BF16)

    pa = _dot(o_ref[...], wa_ref[...])
    pc = _dot(y_ref[...], wc_ref[...])
    out = _sigmoid(ga_ref[...].astype(F32)) * pa + _sigmoid(gc_ref[...].astype(F32)) * pc
    out_ref[...] = out.astype(BF16)


def _merge(o_n, u, u_meta, w_conv, wa_bf, wc_bf, *, seq, d, tm, tn):
    rows = o_n.shape[0]
    cd = d // 2
    hb = BF16_SUBLANES
    kern = functools.partial(_merge_kernel, tiles_per_seq=seq // tm)
    halo = lambda c: pl.BlockSpec((hb, cd), lambda i, j: (jnp.maximum(i * (tm // hb) - 1, 0), c))
    return pl.pallas_call(
        kern,
        grid=(rows // tm, d // tn),
        in_specs=[
            pl.BlockSpec((tm, cd), lambda i, j: (i, 0)),
            pl.BlockSpec((tm, cd), lambda i, j: (i, 3)),
            pl.BlockSpec((tm, cd), lambda i, j: (i, 4)),
            pl.BlockSpec((tm, cd), lambda i, j: (i, 5)),
            halo(3),
            halo(4),
            pl.BlockSpec((hb, cd), lambda i, j: (0, 3)),
            pl.BlockSpec((hb, cd), lambda i, j: (0, 4)),
            pl.BlockSpec((3, cd), lambda i, j: (0, 0)),
            pl.BlockSpec((tm, tn), lambda i, j: (i, 3 * d // tn + j)),
            pl.BlockSpec((tm, tn), lambda i, j: (i, 4 * d // tn + j)),
            pl.BlockSpec((cd, tn), lambda i, j: (0, j)),
            pl.BlockSpec((cd, tn), lambda i, j: (0, j)),
        ],
        out_specs=pl.BlockSpec((tm, tn), lambda i, j: (i, j)),
        out_shape=jax.ShapeDtypeStruct((rows, d), BF16),
        scratch_shapes=[pltpu.VMEM((tm, cd), BF16)],
        compiler_params=_params("arbitrary", "arbitrary"),
        name="merge",
    )(o_n, u, u, u, u, u, u_meta, u_meta, w_conv, u, u, wa_bf, wc_bf)


def _out_router_kernel(mg_ref, h_ref, wo_ref, g_ref, b_ref, wr_ref, rb_ref, tri_ref,
                       h1_ref, h1s_ref, eidx_ref, pos_ref, wtk_ref, counts_ref, cnt_ref, stage_ref, *, alpha):
    @pl.when(pl.program_id(0) == 0)
    def _():
        cnt_ref[...] = jnp.zeros_like(cnt_ref)

    m = _dot(mg_ref[...], wo_ref[...])
    h1 = _layer_norm(alpha * h_ref[...] + m, g_ref[...], b_ref[...])
    h1_ref[...] = h1
    _store_slabs(h1s_ref, h1, stage_ref)

    logits = _dot_nt(wr_ref[...], h1, precision=lax.Precision.HIGHEST)
    s = _sigmoid(logits)
    sel = s + rb_ref[...]
    n_exp, tm = sel.shape
    per_group = n_exp // N_GROUPS
    neg = -jnp.inf

    grow = lax.broadcasted_iota(jnp.int32, (per_group, tm), 0).astype(F32)
    scores = []
    for g in range(N_GROUPS):
        sg = sel[g * per_group:(g + 1) * per_group]
        m1 = jnp.max(sg, axis=0, keepdims=True)
        first = jnp.min(jnp.where(sg == m1, grow, float(per_group)), axis=0, keepdims=True)
        m2 = jnp.max(jnp.where(grow == first, neg, sg), axis=0, keepdims=True)
        scores.append(m1 + m2)
    gs = jnp.concatenate(scores, axis=0)

    gidx = lax.broadcasted_iota(jnp.int32, gs.shape, 0)
    grank = jnp.zeros(gs.shape, F32)
    for g in range(N_GROUPS):
        o = gs[g:g + 1]
        grank = grank + jnp.where((o > gs) | ((o == gs) & (gidx > g)), 1.0, 0.0)
    gkeep = jnp.where(grank < TOPK_GROUPS, 1.0, 0.0)
    keep = jnp.concatenate([jnp.broadcast_to(gkeep[g:g + 1], (per_group, tm)) for g in range(N_GROUPS)], axis=0)
    selm = jnp.where(keep > 0.5, sel, neg)

    eidx = lax.broadcasted_iota(jnp.int32, selm.shape, 0)
    rank = jnp.zeros(selm.shape, F32)
    for e in range(n_exp):
        o = selm[e:e + 1]
        rank = rank + jnp.where((o > selm) | ((o == selm) & (eidx > e)), 1.0, 0.0)
    chosen = rank < TOP_K
    ssel = jnp.where(chosen, s, 0.0)
    w = ssel / jnp.sum(ssel, axis=0, keepdims=True) * ROUTED_SCALE

    chosen_f = jnp.where(chosen, 1.0, 0.0)
    before = cnt_ref[...]
    pos = _dot(chosen_f.astype(BF16), tri_ref[...]) + before
    cnt_ref[...] = before + jnp.sum(chosen_f, axis=1, keepdims=True)
    counts_ref[...] = cnt_ref[...]

    def pick(v):
        rows = [jnp.sum(jnp.where(rank == float(r), v, 0.0), axis=0, keepdims=True) for r in range(TOP_K)]
        return jnp.concatenate(rows, axis=0)

    eidx_ref[...] = pick(eidx.astype(F32)).astype(jnp.int32)
    pos_ref[...] = pick(pos).astype(jnp.int32)
    wtk_ref[...] = pick(w)


def _out_router(merged, h, wo_bf, g, b, w_router_t, router_bias, *, tm, alpha, row0, rows):
    d = merged.shape[1]
    first = row0 // tm
    n_exp = w_router_t.shape[0]
    n_slab = d // LANES
    tri = jnp.triu(jnp.ones((tm, tm), BF16), 1)
    kern = functools.partial(_out_router_kernel, alpha=alpha)
    per_tok = lambda dt: jax.ShapeDtypeStruct((TOP_K, rows), dt)
    return pl.pallas_call(
        kern,
        grid=(rows // tm,),
        in_specs=[
            pl.BlockSpec((tm, d), lambda i: (first + i, 0)),
            pl.BlockSpec((tm, d), lambda i: (first + i, 0)),
            pl.BlockSpec((d, d), lambda i: (0, 0)),
            pl.BlockSpec((1, d), lambda i: (0, 0)),
            pl.BlockSpec((1, d), lambda i: (0, 0)),
            pl.BlockSpec((n_exp, d), lambda i: (0, 0)),
            pl.BlockSpec((n_exp, 1), lambda i: (0, 0)),
            pl.BlockSpec((tm, tm), lambda i: (0, 0)),
        ],
        out_specs=[
            pl.BlockSpec((tm, d), lambda i: (i, 0)),
            pl.BlockSpec((tm * n_slab, LANES), lambda i: (i, 0)),
            pl.BlockSpec((TOP_K, tm), lambda i: (0, i)),
            pl.BlockSpec((TOP_K, tm), lambda i: (0, i)),
            pl.BlockSpec((TOP_K, tm), lambda i: (0, i)),
            pl.BlockSpec((n_exp, 1), lambda i: (0, 0)),
        ],
        out_shape=[
            jax.ShapeDtypeStruct((rows, d), F32),
            jax.ShapeDtypeStruct((rows * n_slab, LANES), F32),
            per_tok(jnp.int32),
            per_tok(jnp.int32),
            per_tok(F32),
            jax.ShapeDtypeStruct((n_exp, 1), F32),
        ],
        scratch_shapes=[pltpu.VMEM((n_exp, 1), F32), pltpu.VMEM((tm * n_slab, LANES), F32)],
        compiler_params=_params("arbitrary"),
        name="out_router",
    )(merged, h, wo_bf, g, b, w_router_t, router_bias, tri)


def _expert_kernel(be_ref, nv_ref, xs_ref, wg_ref, wu_ref, wd_ref, ys_ref, wg_bf, wu_bf, wd_bf, slot_ref,
                   stage_in, stage_out):
    s = pl.program_id(0)
    last = pl.num_programs(0) - 2
    cur = be_ref[jnp.maximum(s - 1, 0)]
    nxt = be_ref[jnp.minimum(s, last)]

    @pl.when(s == 0)
    def _():
        slot_ref[0] = 0

    slot = slot_ref[0]
    fill = jnp.where(s == 0, 0, 1 - slot)

    @pl.when((s == 0) | (nxt != cur))
    def _():
        wg_bf[fill] = wg_ref[...].astype(BF16)
        wu_bf[fill] = wu_ref[...].astype(BF16)
        wd_bf[fill] = wd_ref[...].astype(BF16)

    block = s - 1

    @pl.when((block >= 0) & (block < nv_ref[0]))
    def _():
        d, d_exp = wg_ref.shape
        x = _load_slabs(xs_ref, stage_in, xs_ref.shape[0] * LANES // d, d).astype(BF16)
        g = _dot(x, wg_bf[slot])
        u = _dot(x, wu_bf[slot])
        hid = g * _sigmoid(g) * u
        _store_slabs(ys_ref, _dot(hid.astype(BF16), wd_bf[slot]), stage_out)

    @pl.when(block >= nv_ref[0])
    def _():
        ys_ref[...] = jnp.zeros_like(ys_ref)

    @pl.when((s > 0) & (nxt != cur))
    def _():
        slot_ref[0] = 1 - slot


def _experts(block_e, n_valid, xs, w_gate, w_up, w_down, *, bm):
    _, d, d_exp = w_gate.shape
    n_slab = d // LANES
    nb = xs.shape[0] // (bm * n_slab)
    ahead = lambda s, be, nv: (be[jnp.minimum(s, nb - 1)], 0, 0)
    grid_spec = pltpu.PrefetchScalarGridSpec(
        num_scalar_prefetch=2,
        grid=(nb + 1,),
        in_specs=[
            pl.BlockSpec((bm * n_slab, LANES), lambda s, be, nv: (jnp.maximum(jnp.minimum(s - 1, nv[0] - 1), 0), 0)),
            pl.BlockSpec((None, d, d_exp), ahead),
            pl.BlockSpec((None, d, d_exp), ahead),
            pl.BlockSpec((None, d_exp, d), ahead),
        ],
        out_specs=pl.BlockSpec((bm * n_slab, LANES), lambda s, be, nv: (jnp.maximum(s - 1, 0), 0)),
        scratch_shapes=[
            pltpu.VMEM((2, d, d_exp), BF16),
            pltpu.VMEM((2, d, d_exp), BF16),
            pltpu.VMEM((2, d_exp, d), BF16),
            pltpu.SMEM((1,), jnp.int32),
            pltpu.VMEM((bm * n_slab, LANES), F32),
            pltpu.VMEM((bm * n_slab, LANES), F32),
        ],
    )
    return pl.pallas_call(
        _expert_kernel,
        grid_spec=grid_spec,
        out_shape=jax.ShapeDtypeStruct(xs.shape, BF16),
        compiler_params=_params("arbitrary"),
        name="experts",
    )(block_e, n_valid, xs, w_gate, w_up, w_down)


def _final_kernel(*refs, alpha):
    h1_ref = refs[0]
    y_refs = refs[1:1 + TOP_K]
    w_ref, wg_ref, wu_ref, wd_ref, g_ref, b_ref = refs[1 + TOP_K:7 + TOP_K]
    out_ref, stage_a, stage_b = refs[-3:]
    h1 = h1_ref[...]
    tm, d = h1.shape
    x = h1.astype(BF16)
    gate = _dot(x, wg_ref[...])
    up = _dot(x, wu_ref[...])
    hid = gate * _sigmoid(gate) * up
    f = _dot(hid.astype(BF16), wd_ref[...])
    wt = w_ref[...].T
    for r in range(TOP_K):
        f = f + wt[:, r:r + 1] * _load_slabs(y_refs[r], stage_b if r % 2 else stage_a, tm, d)
    out_ref[...] = _layer_norm(alpha * h1 + f, g_ref[...], b_ref[...])


def _final(h1, yg, w_tk, wg_bf, wu_bf, wd_bf, g, b, *, tm, alpha, row0, total_rows, out_prev):
    rows, d = h1.shape
    d_exp = wg_bf.shape[1]
    n_slab = d // LANES
    tiles = rows // tm
    first = row0 // tm
    kern = functools.partial(_final_kernel, alpha=alpha)
    y_spec = lambda r: pl.BlockSpec((tm * n_slab, LANES), lambda i: (r * tiles + i, 0))
    in_specs = [pl.BlockSpec((tm, d), lambda i: (i, 0))] + [y_spec(r) for r in range(TOP_K)] + [
        pl.BlockSpec((TOP_K, tm), lambda i: (0, i)),
        pl.BlockSpec((d, d_exp), lambda i: (0, 0)),
        pl.BlockSpec((d, d_exp), lambda i: (0, 0)),
        pl.BlockSpec((d_exp, d), lambda i: (0, 0)),
        pl.BlockSpec((1, d), lambda i: (0, 0)),
        pl.BlockSpec((1, d), lambda i: (0, 0)),
    ]
    args = [h1, *([yg] * TOP_K), w_tk, wg_bf, wu_bf, wd_bf, g, b]
    aliases = {}
    if out_prev is not None:
        in_specs.append(pl.BlockSpec(memory_space=pl.ANY))
        aliases = {len(args): 0}
        args.append(out_prev)
    return pl.pallas_call(
        kern,
        grid=(tiles,),
        in_specs=in_specs,
        out_specs=pl.BlockSpec((tm, d), lambda i: (first + i, 0)),
        out_shape=jax.ShapeDtypeStruct((total_rows, d), F32),
        input_output_aliases=aliases,
        scratch_shapes=[pltpu.VMEM((tm * n_slab, LANES), F32)] * 2,
        compiler_params=_params("arbitrary"),
        name="final",
    )(*args)


def _rope_tables(pos, dh):
    half = dh // 2
    inv_freq = ROPE_THETA ** (-jnp.arange(half, dtype=F32) / half)
    ang = pos.astype(F32)[:, None] * inv_freq[None, :]
    cos = jnp.cos(ang)
    sin = jnp.sin(ang)
    reps = LANES // dh
    cos_full = jnp.tile(jnp.concatenate([cos, cos], axis=1), (1, reps))
    sin_full = jnp.tile(jnp.concatenate([-sin, sin], axis=1), (1, reps))
    return cos_full, sin_full


def _tile(n, pref):
    t = min(n, pref)
    assert n % t == 0, (n, pref)
    return t


def kernel(x, meta_tokens, ln0_g, ln0_b, w_in, lambda_q1, lambda_k1, lambda_q2, lambda_k2, subln_g, w_conv, w_proj_attn, w_proj_conv, w_out, ln1_g, ln1_b, w_router, router_bias, w_exp_gate, w_exp_up, w_exp_down, w_sh_gate, w_sh_up, w_sh_down, ln2_g, ln2_b):
    batch, seq, d = x.shape
    n_meta = meta_tokens.shape[0]
    depth = w_in.shape[0]
    dh = lambda_q1.shape[-1]
    n_exp = w_router.shape[-1]
    assert depth == 1 and 2 * dh == LANES and n_meta == BF16_SUBLANES
    assert n_exp // N_GROUPS == 8 and subln_g.shape[-1] == 2 * dh
    cd = d // 2
    heads = cd // (2 * dh)
    n_tok = batch * seq
    alpha = float((2 * depth) ** 0.25)

    row = lambda a: a.reshape(1, -1).astype(F32)

    w_in_bf = w_in[0].astype(BF16)
    cos_m, sin_m = _rope_tables(jnp.arange(n_meta), dh)
    cos_r, sin_r = _rope_tables(jnp.arange(n_meta, n_meta + seq), dh)
    tn_in = _tile(cd, 1024)
    tm_in = _tile(seq, 512)
    inproj = functools.partial(_in_proj, g=row(ln0_g), b=row(ln0_b), w_bf=w_in_bf, tn=tn_in, qk_width=cd, dh=dh)
    u, h = inproj(x.reshape(n_tok, d), cos=cos_r, sin=sin_r, tm=tm_in)
    u_meta, _ = inproj(meta_tokens.astype(F32), cos=cos_m, sin=sin_m, tm=n_meta)

    lamv = jnp.stack([lambda_q1[0], lambda_k1[0], lambda_q2[0], lambda_k2[0]]).astype(F32)
    o_n = _attention(u, u_meta, lamv, row(subln_g[0]), batch=batch, seq=seq, heads=heads, dh=dh,
                     tq=_tile(seq, 512))

    merged = _merge(o_n, u, u_meta, w_conv[0].astype(F32), w_proj_attn[0].astype(BF16),
                    w_proj_conv[0].astype(BF16), seq=seq, d=d, tm=_tile(seq, 512), tn=_tile(d, 1024))

    wo_bf = w_out[0].astype(BF16)
    w_router_t = w_router[0].T.astype(F32)
    router_bias_col = router_bias[0].reshape(n_exp, 1).astype(F32)
    shared_bf = (w_sh_gate[0].astype(BF16), w_sh_up[0].astype(BF16), w_sh_down[0].astype(BF16))

    def moe_group(row0, n_tok, out_prev):
        h1, h1s, eidx, pos_tk, w_tk, counts = _out_router(
            merged, h, wo_bf, row(ln1_g[0]), row(ln1_b[0]), w_router_t, router_bias_col,
            tm=_tile(seq, 256), alpha=alpha, row0=row0, rows=n_tok)

        bm = 256
        n_slab = d // LANES
        n_asg = n_tok * TOP_K
        nb = -(-n_asg // bm) + n_exp
        counts = counts[:, 0].astype(jnp.int32)
        ends = jnp.cumsum(counts)
        starts = ends - counts
        padded = (counts + bm - 1) // bm * bm
        pad_end = jnp.cumsum(padded)
        pad_start = pad_end - padded
        experts = jnp.arange(n_exp, dtype=jnp.int32)
        onehot = eidx[..., None] == experts
        lookup = lambda table: jnp.sum(jnp.where(onehot, table, 0), axis=-1)
        dense_tk = pos_tk + lookup(starts)
        dest_tk = pos_tk + lookup(pad_start)
        tok = jnp.broadcast_to(jnp.arange(n_tok, dtype=jnp.int32)[None], dense_tk.shape)
        _, sorted_tok = lax.sort((dense_tk.reshape(-1), tok.reshape(-1)), num_keys=1)
        block_row = jnp.arange(nb, dtype=jnp.int32) * bm
        block_e = jnp.minimum(jnp.sum(pad_end[None, :] <= block_row[:, None], axis=1), n_exp - 1).astype(jnp.int32)
        n_valid = (pad_end[-1:] // bm).astype(jnp.int32)
        block_hot = block_e[:, None] == experts
        block_lookup = lambda table: jnp.sum(jnp.where(block_hot, table, 0), axis=-1)
        in_expert = (block_row - block_lookup(pad_start))[:, None] + jnp.arange(bm, dtype=jnp.int32)
        dense_row = jnp.minimum(block_lookup(starts)[:, None] + in_expert, n_asg - 1)
        spread = (block_row[:, None] + jnp.arange(bm, dtype=jnp.int32)) % n_tok
        row_tok = jnp.where(in_expert < block_lookup(counts)[:, None],
                            sorted_tok.at[dense_row].get(mode="promise_in_bounds"), spread).reshape(-1)

        xs = h1s.reshape(n_tok, n_slab, LANES).at[row_tok].get(mode="promise_in_bounds")
        ys = _experts(block_e, n_valid, xs.reshape(nb * bm * n_slab, LANES),
                      w_exp_gate[0], w_exp_up[0], w_exp_down[0], bm=bm)
        yg = ys.reshape(nb * bm, n_slab, LANES).at[dest_tk.reshape(-1)].get(mode="promise_in_bounds")

        return _final(h1, yg.reshape(n_asg * n_slab, LANES), w_tk, *shared_bf, row(ln2_g[0]), row(ln2_b[0]),
                      tm=_tile(seq, 256), alpha=alpha, row0=row0, total_rows=batch * seq, out_prev=out_prev)

    n_groups = 1
    group_rows = batch * seq // n_groups
    out = None
    for gi in range(n_groups):
        out = moe_group(gi * group_rows, group_rows, out)
    return out.reshape(batch, seq, d)
```

```python
import functools

import jax
import jax.numpy as jnp
import numpy as np
from jax import lax
from jax.experimental import pallas as pl
from jax.experimental.pallas import tpu as pltpu

CHUNK = 64
ROPE_THETA = 10000.0
LN_EPS = 1e-5
TOP_K = 8
N_GROUPS = 8
TOPK_GROUPS = 4
ROUTED_SCALE = 2.5
LAM_INIT = 0.2
LOG2E = 1.4426950408889634

LANES = 128
BF16_SUBLANES = 16
VMEM_LIMIT_BYTES = 56 * 1024 * 1024

F32 = jnp.float32
BF16 = jnp.bfloat16


def _params(*sem):
    return pltpu.CompilerParams(dimension_semantics=sem, vmem_limit_bytes=VMEM_LIMIT_BYTES)


def _layer_norm(x, g, b):
    mu = jnp.mean(x, axis=-1, keepdims=True)
    xc = x - mu
    var = jnp.mean(xc * xc, axis=-1, keepdims=True)
    return xc * lax.rsqrt(var + LN_EPS) * g + b


def _dot(a, b):
    return jnp.dot(a, b, preferred_element_type=F32)


def _dot_nt(a, b, **kw):
    return lax.dot_general(a, b, (((1,), (1,)), ((), ())), preferred_element_type=F32, **kw)


def _sigmoid(x):
    return 1.0 / (1.0 + jnp.exp(-x))


def _store_slabs(ref, x, stage):
    rows, d = x.shape
    n_slab = d // LANES
    dst = ref if ref.dtype == F32 else stage
    for c in range(n_slab):
        dst[pl.ds(c, rows, stride=n_slab), :] = x[:, c * LANES:(c + 1) * LANES]
    if dst is not ref:
        ref[...] = stage[...].astype(ref.dtype)


def _slab_rows(ref, rows, d):
    n_slab = d // LANES
    return jnp.concatenate([ref[pl.ds(c, rows, stride=n_slab), :] for c in range(n_slab)], axis=1)


def _inproj_kernel(x_ref, g_ref, b_ref, w_ref, cos_ref, sin_ref, u_ref, h_ref, hb_ref,
                   *, q_tiles, qk_scale, half):
    j = pl.program_id(1)

    @pl.when(j == 0)
    def _():
        h = _layer_norm(x_ref[...], g_ref[...], b_ref[...])
        h_ref[...] = h
        hb_ref[...] = h.astype(BF16)

    acc = _dot(hb_ref[...], w_ref[...])
    tn = acc.shape[1]

    @pl.when(j < 2 * q_tiles)
    def _():
        cos = cos_ref[...]
        sin = sin_ref[...]
        lane = lax.broadcasted_iota(jnp.int32, cos.shape, 1)
        first = (lane % (2 * half)) < half
        scale = jnp.where(j < q_tiles, qk_scale, 1.0).astype(F32)
        for c in range(tn // LANES):
            a = acc[:, c * LANES:(c + 1) * LANES]
            partner = jnp.where(first, pltpu.roll(a, LANES - half, axis=1), pltpu.roll(a, half, axis=1))
            u_ref[:, c * LANES:(c + 1) * LANES] = ((a * cos + partner * sin) * scale).astype(BF16)

    @pl.when(j >= 2 * q_tiles)
    def _():
        u_ref[...] = acc.astype(BF16)


def _in_proj(x2d, g, b, w_bf, cos, sin, *, tm, tn, qk_width, dh):
    rows, d = x2d.shape
    cols = w_bf.shape[1]
    pos_blocks = cos.shape[0] // tm
    kern = functools.partial(_inproj_kernel, q_tiles=qk_width // tn, qk_scale=float(dh) ** -0.5 * LOG2E, half=dh // 2)
    return pl.pallas_call(
        kern,
        grid=(rows // tm, cols // tn),
        in_specs=[
            pl.BlockSpec((tm, d), lambda i, j: (i, 0)),
            pl.BlockSpec((1, d), lambda i, j: (0, 0)),
            pl.BlockSpec((1, d), lambda i, j: (0, 0)),
            pl.BlockSpec((d, tn), lambda i, j: (0, j)),
            pl.BlockSpec((tm, LANES), lambda i, j: (i % pos_blocks, 0)),
            pl.BlockSpec((tm, LANES), lambda i, j: (i % pos_blocks, 0)),
        ],
        out_specs=[
            pl.BlockSpec((tm, tn), lambda i, j: (i, j)),
            pl.BlockSpec((tm, d), lambda i, j: (i, 0)),
        ],
        out_shape=[
            jax.ShapeDtypeStruct((rows, cols), BF16),
            jax.ShapeDtypeStruct((rows, d), F32),
        ],
        scratch_shapes=[pltpu.VMEM((tm, d), BF16)],
        compiler_params=_params("arbitrary", "arbitrary"),
        name="in_proj",
    )(x2d, g, b, w_bf, cos, sin)


def _attn_kernel(q_ref, k_ref, v_ref, km_ref, vm_ref, lamv_ref, g_ref, o_ref, vt_ref, vmt_ref, s_ref, *, tq, dh):
    qi = pl.program_id(2)
    seq = k_ref.shape[0]

    @pl.when(qi == 0)
    def _():
        for c in range(seq // tq):
            vt_ref[:, c * tq:(c + 1) * tq] = v_ref[c * tq:(c + 1) * tq, :].astype(F32).T.astype(BF16)
        vmt_ref[...] = vm_ref[...].astype(F32).T.astype(BF16)

    qt = q_ref[...].astype(F32).T
    dim = lax.broadcasted_iota(jnp.int32, qt.shape, 0)
    qq = jnp.concatenate([jnp.where(dim < dh, qt, 0.0), jnp.where(dim >= dh, qt, 0.0)], axis=1).astype(BF16)

    s = _dot(km_ref[...], qq)
    m = jnp.max(s, axis=0, keepdims=True)
    p = jnp.exp2(s - m)
    l = jnp.sum(p, axis=0, keepdims=True)
    acc = _dot(vmt_ref[...], p.astype(BF16))

    def update(carry, s, vt):
        m, l, acc = carry
        m_new = jnp.maximum(m, jnp.max(s, axis=0, keepdims=True))
        a = jnp.exp2(m - m_new)
        p = jnp.exp2(s - m_new)
        l = a * l + jnp.sum(p, axis=0, keepdims=True)
        acc = a * acc + _dot(vt, p.astype(BF16))
        return m_new, l, acc

    def scores(kb):
        return _dot(k_ref[pl.ds(pl.multiple_of(kb * tq, tq), tq), :], qq)

    def values(kb):
        return vt_ref[:, pl.ds(pl.multiple_of(kb * tq, tq), tq)]

    def block_pair(i, carry):
        s_ref[1] = scores(2 * i + 1)
        carry = update(carry, s_ref[0], values(2 * i))
        s_ref[0] = scores(2 * i + 2)
        return update(carry, s_ref[1], values(2 * i + 1))

    def odd_block(carry):
        s_ref[1] = scores(qi)
        carry = update(carry, s_ref[0], values(qi - 1))
        s_ref[0] = s_ref[1]
        return carry

    s_ref[0] = scores(0)
    carry = lax.fori_loop(0, qi // 2, block_pair, (m, l, acc))
    m, l, acc = lax.cond(qi % 2 == 1, odd_block, lambda c: c, carry)
    s = s_ref[0]

    start = pl.multiple_of(qi * tq, tq)
    key = lax.broadcasted_iota(jnp.int32, s.shape, 0)
    qry = lax.broadcasted_iota(jnp.int32, s.shape, 1)
    qry = jnp.where(qry >= tq, qry - tq, qry)
    s = jnp.where((key // CHUNK) <= (qry // CHUNK), s, -jnp.inf)
    m, l, acc = update((m, l, acc), s, vt_ref[:, pl.ds(start, tq)])

    lamv = lamv_ref[...]
    lam = (jnp.exp(jnp.sum(lamv[0:1] * lamv[1:2], axis=1, keepdims=True))
           - jnp.exp(jnp.sum(lamv[2:3] * lamv[3:4], axis=1, keepdims=True)) + LAM_INIT)
    o_all = acc / l
    o = (o_all[:, :tq] - lam * o_all[:, tq:]).T
    ms = jnp.mean(o * o, axis=-1, keepdims=True)
    o = o * lax.rsqrt(ms + LN_EPS) * g_ref[...] * (1.0 - LAM_INIT)
    o_ref[...] = o.astype(BF16)


def _attention(u, u_meta, lamv, subln_g, *, batch, seq, heads, dh, tq):
    vd = 2 * dh
    nq = seq // tq
    kern = functools.partial(_attn_kernel, tq=tq, dh=dh)
    n_meta = u_meta.shape[0]
    return pl.pallas_call(
        kern,
        grid=(batch, heads, nq),
        in_specs=[
            pl.BlockSpec((tq, vd), lambda b, h, i: (b * nq + i, h)),
            pl.BlockSpec((seq, vd), lambda b, h, i: (b, heads + h)),
            pl.BlockSpec((seq, vd), lambda b, h, i: (b, 2 * heads + h)),
            pl.BlockSpec((n_meta, vd), lambda b, h, i: (0, heads + h)),
            pl.BlockSpec((n_meta, vd), lambda b, h, i: (0, 2 * heads + h)),
            pl.BlockSpec((4, dh), lambda b, h, i: (0, 0)),
            pl.BlockSpec((1, vd), lambda b, h, i: (0, 0)),
        ],
        out_specs=pl.BlockSpec((tq, vd), lambda b, h, i: (b * nq + i, h)),
        out_shape=jax.ShapeDtypeStruct((batch * seq, heads * vd), BF16),
        scratch_shapes=[pltpu.VMEM((vd, seq), BF16), pltpu.VMEM((vd, n_meta), BF16),
                        pltpu.VMEM((2, tq, 2 * tq), F32)],
        compiler_params=_params("arbitrary", "arbitrary", "arbitrary"),
        name="attention",
    )(u, u, u, u_meta, u_meta, lamv, subln_g)


def _merge_kernel(o_ref, cx_ref, cc_ref, cb_ref, hx_ref, hc_ref, mx_ref, mc_ref, wconv_ref,
                  ga_ref, gc_ref, wa_ref, wc_ref, out_ref, y_ref, *, tiles_per_seq):
    i = pl.program_id(0)
    j = pl.program_id(1)

    @pl.when(j == 0)
    def _():
        z = cc_ref[...].astype(F32) * cx_ref[...].astype(F32)
        cb = cb_ref[...].astype(F32)
        w = wconv_ref[...]
        w0, w1, w2 = w[0:1], w[1:2], w[2:3]
        y_ref[...] = (cb * (w0 * pltpu.roll(z, 2, axis=0) + w1 * pltpu.roll(z, 1, axis=0) + w2 * z)).astype(BF16)
        hb = BF16_SUBLANES
        first = (i % tiles_per_seq) == 0
        hz_prev = hc_ref[...].astype(F32) * hx_ref[...].astype(F32)
        hz_meta = mc_ref[...].astype(F32) * mx_ref[...].astype(F32)
        hz = jnp.where(first, hz_meta, hz_prev)
        zm1 = hz[hb - 1:hb]
        zm2 = hz[hb - 2:hb - 1]
        zh = z[0:hb]
        row = lax.broadcasted_iota(jnp.int32, zh.shape, 0)
        z1 = jnp.where(row == 0, zm1, pltpu.roll(zh, 1, axis=0))
        z2 = jnp.where(row == 0, zm2, jnp.where(row == 1, zm1, pltpu.roll(zh, 2, axis=0)))
        y_ref[0:hb, :] = (cb[0:hb] * (w0 * z2 + w1 * z1 + w2 * zh)).astype(BF16)

    pa = _dot(o_ref[...], wa_ref[...])
    pc = _dot(y_ref[...], wc_ref[...])
    out = _sigmoid(ga_ref[...].astype(F32)) * pa + _sigmoid(gc_ref[...].astype(F32)) * pc
    out_ref[...] = out.astype(BF16)


def _merge(o_n, u, u_meta, w_conv, wa_bf, wc_bf, *, seq, d, tm, tn):
    rows = o_n.shape[0]
    cd = d // 2
    hb = BF16_SUBLANES
    kern = functools.partial(_merge_kernel, tiles_per_seq=seq // tm)
    halo = lambda c: pl.BlockSpec((hb, cd), lambda i, j: (jnp.maximum(i * (tm // hb) - 1, 0), c))
    return pl.pallas_call(
        kern,
        grid=(rows // tm, d // tn),
        in_specs=[
            pl.BlockSpec((tm, cd), lambda i, j: (i, 0)),
            pl.BlockSpec((tm, cd), lambda i, j: (i, 3)),
            pl.BlockSpec((tm, cd), lambda i, j: (i, 4)),
            pl.BlockSpec((tm, cd), lambda i, j: (i, 5)),
            halo(3),
            halo(4),
            pl.BlockSpec((hb, cd), lambda i, j: (0, 3)),
            pl.BlockSpec((hb, cd), lambda i, j: (0, 4)),
            pl.BlockSpec((3, cd), lambda i, j: (0, 0)),
            pl.BlockSpec((tm, tn), lambda i, j: (i, 3 * d // tn + j)),
            pl.BlockSpec((tm, tn), lambda i, j: (i, 4 * d // tn + j)),
            pl.BlockSpec((cd, tn), lambda i, j: (0, j)),
            pl.BlockSpec((cd, tn), lambda i, j: (0, j)),
        ],
        out_specs=pl.BlockSpec((tm, tn), lambda i, j: (i, j)),
        out_shape=jax.ShapeDtypeStruct((rows, d), BF16),
        scratch_shapes=[pltpu.VMEM((tm, cd), BF16)],
        compiler_params=_params("arbitrary", "arbitrary"),
        name="merge",
    )(o_n, u, u, u, u, u, u_meta, u_meta, w_conv, u, u, wa_bf, wc_bf)


def _out_router_kernel(mg_ref, h_ref, wo_ref, g_ref, b_ref, wr_ref, rb_ref, tri_ref,
                       h1_ref, h1s_ref, eidx_ref, pos_ref, wtk_ref, counts_ref, cnt_ref, stage_ref, *, alpha):
    @pl.when(pl.program_id(0) == 0)
    def _():
        cnt_ref[...] = jnp.zeros_like(cnt_ref)

    m = _dot(mg_ref[...], wo_ref[...])
    h1 = _layer_norm(alpha * h_ref[...] + m, g_ref[...], b_ref[...])
    h1_ref[...] = h1
    _store_slabs(h1s_ref, h1, stage_ref)

    logits = _dot_nt(wr_ref[...], h1, precision=lax.Precision.HIGHEST)
    s = _sigmoid(logits)
    sel = s + rb_ref[...]
    n_exp, tm = sel.shape
    per_group = n_exp // N_GROUPS
    neg = -jnp.inf

    grow = lax.broadcasted_iota(jnp.int32, (per_group, tm), 0).astype(F32)
    scores = []
    for g in range(N_GROUPS):
        sg = sel[g * per_group:(g + 1) * per_group]
        m1 = jnp.max(sg, axis=0, keepdims=True)
        first = jnp.min(jnp.where(sg == m1, grow, float(per_group)), axis=0, keepdims=True)
        m2 = jnp.max(jnp.where(grow == first, neg, sg), axis=0, keepdims=True)
        scores.append(m1 + m2)
    gs = jnp.concatenate(scores, axis=0)

    gidx = lax.broadcasted_iota(jnp.int32, gs.shape, 0)
    grank = jnp.zeros(gs.shape, F32)
    for g in range(N_GROUPS):
        o = gs[g:g + 1]
        grank = grank + jnp.where((o > gs) | ((o == gs) & (gidx > g)), 1.0, 0.0)
    gkeep = jnp.where(grank < TOPK_GROUPS, 1.0, 0.0)
    keep = jnp.concatenate([jnp.broadcast_to(gkeep[g:g + 1], (per_group, tm)) for g in range(N_GROUPS)], axis=0)
    selm = jnp.where(keep > 0.5, sel, neg)

    eidx = lax.broadcasted_iota(jnp.int32, selm.shape, 0)
    rank = jnp.zeros(selm.shape, F32)
    for e in range(n_exp):
        o = selm[e:e + 1]
        rank = rank + jnp.where((o > selm) | ((o == selm) & (eidx > e)), 1.0, 0.0)
    chosen = rank < TOP_K
    ssel = jnp.where(chosen, s, 0.0)
    w = ssel / jnp.sum(ssel, axis=0, keepdims=True) * ROUTED_SCALE

    chosen_f = jnp.where(chosen, 1.0, 0.0)
    before = cnt_ref[...]
    pos = _dot(chosen_f.astype(BF16), tri_ref[...]) + before
    cnt_ref[...] = before + jnp.sum(chosen_f, axis=1, keepdims=True)
    counts_ref[...] = cnt_ref[...]

    def pick(v):
        rows = [jnp.sum(jnp.where(rank == float(r), v, 0.0), axis=0, keepdims=True) for r in range(TOP_K)]
        return jnp.concatenate(rows, axis=0)

    eidx_ref[...] = pick(eidx.astype(F32)).astype(jnp.int32)
    pos_ref[...] = pick(pos).astype(jnp.int32)
    wtk_ref[...] = pick(w)


def _out_router(merged, h, wo_bf, g, b, w_router_t, router_bias, *, tm, alpha):
    rows, d = merged.shape
    n_exp = w_router_t.shape[0]
    n_slab = d // LANES
    tri = jnp.triu(jnp.ones((tm, tm), BF16), 1)
    kern = functools.partial(_out_router_kernel, alpha=alpha)
    per_tok = lambda dt: jax.ShapeDtypeStruct((TOP_K, rows), dt)
    return pl.pallas_call(
        kern,
        grid=(rows // tm,),
        in_specs=[
            pl.BlockSpec((tm, d), lambda i: (i, 0)),
            pl.BlockSpec((tm, d), lambda i: (i, 0)),
            pl.BlockSpec((d, d), lambda i: (0, 0)),
            pl.BlockSpec((1, d), lambda i: (0, 0)),
            pl.BlockSpec((1, d), lambda i: (0, 0)),
            pl.BlockSpec((n_exp, d), lambda i: (0, 0)),
            pl.BlockSpec((n_exp, 1), lambda i: (0, 0)),
            pl.BlockSpec((tm, tm), lambda i: (0, 0)),
        ],
        out_specs=[
            pl.BlockSpec((tm, d), lambda i: (i, 0)),
            pl.BlockSpec((tm * n_slab, LANES), lambda i: (i, 0)),
            pl.BlockSpec((TOP_K, tm), lambda i: (0, i)),
            pl.BlockSpec((TOP_K, tm), lambda i: (0, i)),
            pl.BlockSpec((TOP_K, tm), lambda i: (0, i)),
            pl.BlockSpec((n_exp, 1), lambda i: (0, 0)),
        ],
        out_shape=[
            jax.ShapeDtypeStruct((rows, d), F32),
            jax.ShapeDtypeStruct((rows * n_slab, LANES), F32),
            per_tok(jnp.int32),
            per_tok(jnp.int32),
            per_tok(F32),
            jax.ShapeDtypeStruct((n_exp, 1), F32),
        ],
        scratch_shapes=[pltpu.VMEM((n_exp, 1), F32), pltpu.VMEM((tm * n_slab, LANES), F32)],
        compiler_params=_params("arbitrary"),
        name="out_router",
    )(merged, h, wo_bf, g, b, w_router_t, router_bias, tri)


def _expert_kernel(be_ref, nv_ref, xs_ref, rw_ref, wg_ref, wu_ref, wd_ref, ys_ref, wg_bf, wu_bf, wd_bf, slot_ref,
                   stage_out):
    s = pl.program_id(0)
    last = pl.num_programs(0) - 2
    cur = be_ref[jnp.maximum(s - 1, 0)]
    nxt = be_ref[jnp.minimum(s, last)]

    @pl.when(s == 0)
    def _():
        slot_ref[0] = 0

    slot = slot_ref[0]
    fill = jnp.where(s == 0, 0, 1 - slot)

    @pl.when((s == 0) | (nxt != cur))
    def _():
        wg_bf[fill] = wg_ref[...].astype(BF16)
        wu_bf[fill] = wu_ref[...].astype(BF16)
        wd_bf[fill] = wd_ref[...].astype(BF16)

    block = s - 1

    @pl.when((block >= 0) & (block < nv_ref[0]))
    def _():
        d, d_exp = wg_ref.shape
        x = _slab_rows(xs_ref, xs_ref.shape[0] * LANES // d, d).astype(BF16)
        g = _dot(x, wg_bf[slot])
        u = _dot(x, wu_bf[slot])
        hid = g * _sigmoid(g) * u
        _store_slabs(ys_ref, _dot(hid.astype(BF16), wd_bf[slot]) * rw_ref[...], stage_out)

    @pl.when(block >= nv_ref[0])
    def _():
        ys_ref[...] = jnp.zeros_like(ys_ref)

    @pl.when((s > 0) & (nxt != cur))
    def _():
        slot_ref[0] = 1 - slot


def _experts(block_e, n_valid, xs, row_w, w_gate, w_up, w_down, *, bm):
    _, d, d_exp = w_gate.shape
    n_slab = d // LANES
    nb = xs.shape[0] // (bm * n_slab)
    ahead = lambda s, be, nv: (be[jnp.minimum(s, nb - 1)], 0, 0)
    grid_spec = pltpu.PrefetchScalarGridSpec(
        num_scalar_prefetch=2,
        grid=(nb + 1,),
        in_specs=[
            pl.BlockSpec((bm * n_slab, LANES), lambda s, be, nv: (jnp.maximum(jnp.minimum(s - 1, nv[0] - 1), 0), 0)),
            pl.BlockSpec((bm, 1), lambda s, be, nv: (jnp.maximum(jnp.minimum(s - 1, nv[0] - 1), 0), 0)),
            pl.BlockSpec((None, d, d_exp), ahead),
            pl.BlockSpec((None, d, d_exp), ahead),
            pl.BlockSpec((None, d_exp, d), ahead),
        ],
        out_specs=pl.BlockSpec((bm * n_slab, LANES), lambda s, be, nv: (jnp.maximum(s - 1, 0), 0)),
        scratch_shapes=[
            pltpu.VMEM((2, d, d_exp), BF16),
            pltpu.VMEM((2, d, d_exp), BF16),
            pltpu.VMEM((2, d_exp, d), BF16),
            pltpu.SMEM((1,), jnp.int32),
            pltpu.VMEM((bm * n_slab, LANES), F32),
        ],
    )
    return pl.pallas_call(
        _expert_kernel,
        grid_spec=grid_spec,
        out_shape=jax.ShapeDtypeStruct(xs.shape, BF16),
        compiler_params=_params("arbitrary"),
        name="experts",
    )(block_e, n_valid, xs, row_w, w_gate, w_up, w_down)


def _shared_kernel(h1_ref, wg_ref, wu_ref, wd_ref, o_ref):
    x = h1_ref[...].astype(BF16)
    gate = _dot(x, wg_ref[...])
    up = _dot(x, wu_ref[...])
    hid = gate * _sigmoid(gate) * up
    o_ref[...] = _dot(hid.astype(BF16), wd_ref[...]).astype(BF16)


def _shared(h1, wg_bf, wu_bf, wd_bf, *, tm):
    rows, d = h1.shape
    d_exp = wg_bf.shape[1]
    return pl.pallas_call(
        _shared_kernel,
        grid=(rows // tm,),
        in_specs=[
            pl.BlockSpec((tm, d), lambda i: (i, 0)),
            pl.BlockSpec((d, d_exp), lambda i: (0, 0)),
            pl.BlockSpec((d, d_exp), lambda i: (0, 0)),
            pl.BlockSpec((d_exp, d), lambda i: (0, 0)),
        ],
        out_specs=pl.BlockSpec((tm, d), lambda i: (i, 0)),
        out_shape=jax.ShapeDtypeStruct((rows, d), BF16),
        compiler_params=_params("arbitrary"),
        name="shared",
    )(h1, wg_bf, wu_bf, wd_bf)


def _final_kernel(*refs, alpha):
    h1_ref, sh_ref = refs[:2]
    y_refs = refs[2:2 + TOP_K]
    g_ref, b_ref, out_ref, stage_ref = refs[2 + TOP_K:]
    h1 = h1_ref[...]
    tm, d = h1.shape
    acc = y_refs[0][...].astype(F32)
    for r in range(1, TOP_K):
        acc = acc + y_refs[r][...].astype(F32)
    stage_ref[...] = acc
    f = sh_ref[...].astype(F32) + _slab_rows(stage_ref, tm, d)
    out_ref[...] = _layer_norm(alpha * h1 + f, g_ref[...], b_ref[...])


def _final(h1, shared, yg, g, b, *, tm, alpha):
    rows, d = h1.shape
    n_slab = d // LANES
    tiles = rows // tm
    kern = functools.partial(_final_kernel, alpha=alpha)
    y_spec = lambda r: pl.BlockSpec((tm * n_slab, LANES), lambda i: (r * tiles + i, 0))
    return pl.pallas_call(
        kern,
        grid=(tiles,),
        in_specs=[pl.BlockSpec((tm, d), lambda i: (i, 0)), pl.BlockSpec((tm, d), lambda i: (i, 0))]
        + [y_spec(r) for r in range(TOP_K)]
        + [pl.BlockSpec((1, d), lambda i: (0, 0)), pl.BlockSpec((1, d), lambda i: (0, 0))],
        out_specs=pl.BlockSpec((tm, d), lambda i: (i, 0)),
        out_shape=jax.ShapeDtypeStruct((rows, d), F32),
        scratch_shapes=[pltpu.VMEM((tm * n_slab, LANES), F32)],
        compiler_params=_params("arbitrary"),
        name="final",
    )(h1, shared, *([yg] * TOP_K), g, b)


def _rope_tables(pos, dh):
    half = dh // 2
    inv_freq = ROPE_THETA ** (-jnp.arange(half, dtype=F32) / half)
    ang = pos.astype(F32)[:, None] * inv_freq[None, :]
    cos = jnp.cos(ang)
    sin = jnp.sin(ang)
    reps = LANES // dh
    cos_full = jnp.tile(jnp.concatenate([cos, cos], axis=1), (1, reps))
    sin_full = jnp.tile(jnp.concatenate([-sin, sin], axis=1), (1, reps))
    return cos_full, sin_full


def _tile(n, pref):
    t = min(n, pref)
    assert n % t == 0, (n, pref)
    return t


def kernel(x, meta_tokens, ln0_g, ln0_b, w_in, lambda_q1, lambda_k1, lambda_q2, lambda_k2, subln_g, w_conv, w_proj_attn, w_proj_conv, w_out, ln1_g, ln1_b, w_router, router_bias, w_exp_gate, w_exp_up, w_exp_down, w_sh_gate, w_sh_up, w_sh_down, ln2_g, ln2_b):
    batch, seq, d = x.shape
    n_meta = meta_tokens.shape[0]
    depth = w_in.shape[0]
    dh = lambda_q1.shape[-1]
    n_exp = w_router.shape[-1]
    assert depth == 1 and 2 * dh == LANES and n_meta == BF16_SUBLANES
    assert n_exp // N_GROUPS == 8 and subln_g.shape[-1] == 2 * dh
    cd = d // 2
    heads = cd // (2 * dh)
    n_tok = batch * seq
    alpha = float((2 * depth) ** 0.25)

    row = lambda a: a.reshape(1, -1).astype(F32)

    w_in_bf = w_in[0].astype(BF16)
    cos_m, sin_m = _rope_tables(jnp.arange(n_meta), dh)
    cos_r, sin_r = _rope_tables(jnp.arange(n_meta, n_meta + seq), dh)
    tn_in = _tile(cd, 1024)
    tm_in = _tile(seq, 512)
    inproj = functools.partial(_in_proj, g=row(ln0_g), b=row(ln0_b), w_bf=w_in_bf, tn=tn_in, qk_width=cd, dh=dh)
    u, h = inproj(x.reshape(n_tok, d), cos=cos_r, sin=sin_r, tm=tm_in)
    u_meta, _ = inproj(meta_tokens.astype(F32), cos=cos_m, sin=sin_m, tm=n_meta)

    lamv = jnp.stack([lambda_q1[0], lambda_k1[0], lambda_q2[0], lambda_k2[0]]).astype(F32)
    o_n = _attention(u, u_meta, lamv, row(subln_g[0]), batch=batch, seq=seq, heads=heads, dh=dh,
                     tq=_tile(seq, 512))

    merged = _merge(o_n, u, u_meta, w_conv[0].astype(F32), w_proj_attn[0].astype(BF16),
                    w_proj_conv[0].astype(BF16), seq=seq, d=d, tm=_tile(seq, 512), tn=_tile(d, 1024))

    h1, h1s, eidx, pos_tk, w_tk, counts = _out_router(
        merged, h, w_out[0].astype(BF16), row(ln1_g[0]), row(ln1_b[0]),
        w_router[0].T.astype(F32), router_bias[0].reshape(n_exp, 1).astype(F32),
        tm=_tile(seq, 256), alpha=alpha)

    bm = 256
    n_slab = d // LANES
    n_asg = n_tok * TOP_K
    nb = -(-n_asg // bm) + n_exp
    counts = counts[:, 0].astype(jnp.int32)
    ends = jnp.cumsum(counts)
    starts = ends - counts
    padded = (counts + bm - 1) // bm * bm
    pad_end = jnp.cumsum(padded)
    pad_start = pad_end - padded
    experts = jnp.arange(n_exp, dtype=jnp.int32)
    onehot = eidx[..., None] == experts
    lookup = lambda table: jnp.sum(jnp.where(onehot, table, 0), axis=-1)
    dense_tk = pos_tk + lookup(starts)
    dest_tk = pos_tk + lookup(pad_start)
    tok = jnp.broadcast_to(jnp.arange(n_tok, dtype=jnp.int32)[None], dense_tk.shape)
    _, sorted_tok, sorted_w = lax.sort((dense_tk.reshape(-1), tok.reshape(-1), w_tk.reshape(-1)), num_keys=1)
    block_row = jnp.arange(nb, dtype=jnp.int32) * bm
    block_e = jnp.minimum(jnp.sum(pad_end[None, :] <= block_row[:, None], axis=1), n_exp - 1).astype(jnp.int32)
    n_valid = (pad_end[-1:] // bm).astype(jnp.int32)
    block_hot = block_e[:, None] == experts
    block_lookup = lambda table: jnp.sum(jnp.where(block_hot, table, 0), axis=-1)
    in_expert = (block_row - block_lookup(pad_start))[:, None] + jnp.arange(bm, dtype=jnp.int32)
    dense_row = jnp.minimum(block_lookup(starts)[:, None] + in_expert, n_asg - 1)
    real = in_expert < block_lookup(counts)[:, None]
    spread = (block_row[:, None] + jnp.arange(bm, dtype=jnp.int32)) % n_tok
    row_tok = jnp.where(real, sorted_tok.at[dense_row].get(mode="promise_in_bounds"), spread).reshape(-1)
    row_w = jnp.where(real, sorted_w.at[dense_row].get(mode="promise_in_bounds"), 0.0).reshape(-1, 1)

    xs = h1s.reshape(n_tok, n_slab, LANES).at[row_tok].get(mode="promise_in_bounds")
    shared = _shared(h1, w_sh_gate[0].astype(BF16), w_sh_up[0].astype(BF16), w_sh_down[0].astype(BF16),
                     tm=_tile(seq, 512))
    ys = _experts(block_e, n_valid, xs.reshape(nb * bm * n_slab, LANES), row_w,
                  w_exp_gate[0], w_exp_up[0], w_exp_down[0], bm=bm)
    yg = ys.reshape(nb * bm, n_slab, LANES).at[dest_tk.reshape(-1)].get(mode="promise_in_bounds")

    out = _final(h1, shared, yg.reshape(n_asg * n_slab, LANES), row(ln2_g[0]), row(ln2_b[0]),
                 tm=_tile(seq, 256), alpha=alpha)
    return out.reshape(batch, seq, d)
```

```python
import functools

import jax
import jax.numpy as jnp
import numpy as np
from jax import lax
from jax.experimental import pallas as pl
from jax.experimental.pallas import tpu as pltpu

CHUNK = 64
ROPE_THETA = 10000.0
LN_EPS = 1e-5
TOP_K = 8
N_GROUPS = 8
TOPK_GROUPS = 4
ROUTED_SCALE = 2.5
LAM_INIT = 0.2
LOG2E = 1.4426950408889634
EXPERT_CHUNKS = 4
COMBINE_CHUNKS = 2

LANES = 128
BF16_SUBLANES = 16
VMEM_LIMIT_BYTES = 56 * 1024 * 1024

F32 = jnp.float32
BF16 = jnp.bfloat16


def _params(*sem):
    return pltpu.CompilerParams(dimension_semantics=sem, vmem_limit_bytes=VMEM_LIMIT_BYTES)


def _layer_norm(x, g, b):
    mu = jnp.mean(x, axis=-1, keepdims=True)
    xc = x - mu
    var = jnp.mean(xc * xc, axis=-1, keepdims=True)
    return xc * lax.rsqrt(var + LN_EPS) * g + b


def _dot(a, b):
    return jnp.dot(a, b, preferred_element_type=F32)


def _dot_nt(a, b, **kw):
    return lax.dot_general(a, b, (((1,), (1,)), ((), ())), preferred_element_type=F32, **kw)


def _sigmoid(x):
    return 1.0 / (1.0 + jnp.exp(-x))


def _store_slabs(ref, x, stage):
    rows, d = x.shape
    n_slab = d // LANES
    dst = ref if ref.dtype == F32 else stage
    for c in range(n_slab):
        dst[pl.ds(c, rows, stride=n_slab), :] = x[:, c * LANES:(c + 1) * LANES]
    if dst is not ref:
        ref[...] = stage[...].astype(ref.dtype)


def _slab_rows(ref, rows, d):
    n_slab = d // LANES
    return jnp.concatenate([ref[pl.ds(c, rows, stride=n_slab), :] for c in range(n_slab)], axis=1)


def _inproj_kernel(x_ref, g_ref, b_ref, w_ref, cos_ref, sin_ref, u_ref, h_ref, hb_ref,
                   *, q_tiles, qk_scale, half):
    j = pl.program_id(1)

    @pl.when(j == 0)
    def _():
        h = _layer_norm(x_ref[...], g_ref[...], b_ref[...])
        h_ref[...] = h
        hb_ref[...] = h.astype(BF16)

    acc = _dot(hb_ref[...], w_ref[...])
    tn = acc.shape[1]

    @pl.when(j < 2 * q_tiles)
    def _():
        cos = cos_ref[...]
        sin = sin_ref[...]
        lane = lax.broadcasted_iota(jnp.int32, cos.shape, 1)
        first = (lane % (2 * half)) < half
        scale = jnp.where(j < q_tiles, qk_scale, 1.0).astype(F32)
        for c in range(tn // LANES):
            a = acc[:, c * LANES:(c + 1) * LANES]
            partner = jnp.where(first, pltpu.roll(a, LANES - half, axis=1), pltpu.roll(a, half, axis=1))
            u_ref[:, c * LANES:(c + 1) * LANES] = ((a * cos + partner * sin) * scale).astype(BF16)

    @pl.when(j >= 2 * q_tiles)
    def _():
        u_ref[...] = acc.astype(BF16)


def _in_proj(x2d, g, b, w_bf, cos, sin, *, tm, tn, qk_width, dh):
    rows, d = x2d.shape
    cols = w_bf.shape[1]
    pos_blocks = cos.shape[0] // tm
    kern = functools.partial(_inproj_kernel, q_tiles=qk_width // tn, qk_scale=float(dh) ** -0.5 * LOG2E, half=dh // 2)
    return pl.pallas_call(
        kern,
        grid=(rows // tm, cols // tn),
        in_specs=[
            pl.BlockSpec((tm, d), lambda i, j: (i, 0)),
            pl.BlockSpec((1, d), lambda i, j: (0, 0)),
            pl.BlockSpec((1, d), lambda i, j: (0, 0)),
            pl.BlockSpec((d, tn), lambda i, j: (0, j)),
            pl.BlockSpec((tm, LANES), lambda i, j: (i % pos_blocks, 0)),
            pl.BlockSpec((tm, LANES), lambda i, j: (i % pos_blocks, 0)),
        ],
        out_specs=[
            pl.BlockSpec((tm, tn), lambda i, j: (i, j)),
            pl.BlockSpec((tm, d), lambda i, j: (i, 0)),
        ],
        out_shape=[
            jax.ShapeDtypeStruct((rows, cols), BF16),
            jax.ShapeDtypeStruct((rows, d), F32),
        ],
        scratch_shapes=[pltpu.VMEM((tm, d), BF16)],
        compiler_params=_params("arbitrary", "arbitrary"),
        name="in_proj",
    )(x2d, g, b, w_bf, cos, sin)


def _attn_kernel(q_ref, k_ref, v_ref, km_ref, vm_ref, lamv_ref, g_ref, o_ref, vt_ref, vmt_ref, s_ref, *, tq, dh):
    qi = pl.program_id(2)
    seq = k_ref.shape[0]

    @pl.when(qi == 0)
    def _():
        for c in range(seq // tq):
            vt_ref[:, c * tq:(c + 1) * tq] = v_ref[c * tq:(c + 1) * tq, :].astype(F32).T.astype(BF16)
        vmt_ref[...] = vm_ref[...].astype(F32).T.astype(BF16)

    qt = q_ref[...].astype(F32).T
    dim = lax.broadcasted_iota(jnp.int32, qt.shape, 0)
    qq = jnp.concatenate([jnp.where(dim < dh, qt, 0.0), jnp.where(dim >= dh, qt, 0.0)], axis=1).astype(BF16)

    s = _dot(km_ref[...], qq)
    m = jnp.max(s, axis=0, keepdims=True)
    p = jnp.exp2(s - m)
    l = jnp.sum(p, axis=0, keepdims=True)
    acc = _dot(vmt_ref[...], p.astype(BF16))

    def update(carry, s, vt):
        m, l, acc = carry
        m_new = jnp.maximum(m, jnp.max(s, axis=0, keepdims=True))
        a = jnp.exp2(m - m_new)
        p = jnp.exp2(s - m_new)
        l = a * l + jnp.sum(p, axis=0, keepdims=True)
        acc = a * acc + _dot(vt, p.astype(BF16))
        return m_new, l, acc

    def scores(kb):
        return _dot(k_ref[pl.ds(pl.multiple_of(kb * tq, tq), tq), :], qq)

    def values(kb):
        return vt_ref[:, pl.ds(pl.multiple_of(kb * tq, tq), tq)]

    def block_pair(i, carry):
        s_ref[1] = scores(2 * i + 1)
        carry = update(carry, s_ref[0], values(2 * i))
        s_ref[0] = scores(2 * i + 2)
        return update(carry, s_ref[1], values(2 * i + 1))

    def odd_block(carry):
        s_ref[1] = scores(qi)
        carry = update(carry, s_ref[0], values(qi - 1))
        s_ref[0] = s_ref[1]
        return carry

    s_ref[0] = scores(0)
    carry = lax.fori_loop(0, qi // 2, block_pair, (m, l, acc))
    m, l, acc = lax.cond(qi % 2 == 1, odd_block, lambda c: c, carry)
    s = s_ref[0]

    start = pl.multiple_of(qi * tq, tq)
    key = lax.broadcasted_iota(jnp.int32, s.shape, 0)
    qry = lax.broadcasted_iota(jnp.int32, s.shape, 1)
    qry = jnp.where(qry >= tq, qry - tq, qry)
    s = jnp.where((key // CHUNK) <= (qry // CHUNK), s, -jnp.inf)
    m, l, acc = update((m, l, acc), s, vt_ref[:, pl.ds(start, tq)])

    lamv = lamv_ref[...]
    lam = (jnp.exp(jnp.sum(lamv[0:1] * lamv[1:2], axis=1, keepdims=True))
           - jnp.exp(jnp.sum(lamv[2:3] * lamv[3:4], axis=1, keepdims=True)) + LAM_INIT)
    o_all = acc / l
    o = (o_all[:, :tq] - lam * o_all[:, tq:]).T
    ms = jnp.mean(o * o, axis=-1, keepdims=True)
    o = o * lax.rsqrt(ms + LN_EPS) * g_ref[...] * (1.0 - LAM_INIT)
    o_ref[...] = o.astype(BF16)


def _attention(u, u_meta, lamv, subln_g, *, batch, seq, heads, dh, tq):
    vd = 2 * dh
    nq = seq // tq
    kern = functools.partial(_attn_kernel, tq=tq, dh=dh)
    n_meta = u_meta.shape[0]
    return pl.pallas_call(
        kern,
        grid=(batch, heads, nq),
        in_specs=[
            pl.BlockSpec((tq, vd), lambda b, h, i: (b * nq + i, h)),
            pl.BlockSpec((seq, vd), lambda b, h, i: (b, heads + h)),
            pl.BlockSpec((seq, vd), lambda b, h, i: (b, 2 * heads + h)),
            pl.BlockSpec((n_meta, vd), lambda b, h, i: (0, heads + h)),
            pl.BlockSpec((n_meta, vd), lambda b, h, i: (0, 2 * heads + h)),
            pl.BlockSpec((4, dh), lambda b, h, i: (0, 0)),
            pl.BlockSpec((1, vd), lambda b, h, i: (0, 0)),
        ],
        out_specs=pl.BlockSpec((tq, vd), lambda b, h, i: (b * nq + i, h)),
        out_shape=jax.ShapeDtypeStruct((batch * seq, heads * vd), BF16),
        scratch_shapes=[pltpu.VMEM((vd, seq), BF16), pltpu.VMEM((vd, n_meta), BF16),
                        pltpu.VMEM((2, tq, 2 * tq), F32)],
        compiler_params=_params("arbitrary", "arbitrary", "arbitrary"),
        name="attention",
    )(u, u, u, u_meta, u_meta, lamv, subln_g)


def _merge_kernel(o_ref, cx_ref, cc_ref, cb_ref, hx_ref, hc_ref, mx_ref, mc_ref, wconv_ref,
                  ga_ref, gc_ref, wa_ref, wc_ref, out_ref, y_ref, *, tiles_per_seq):
    i = pl.program_id(0)
    j = pl.program_id(1)

    @pl.when(j == 0)
    def _():
        z = cc_ref[...].astype(F32) * cx_ref[...].astype(F32)
        cb = cb_ref[...].astype(F32)
        w = wconv_ref[...]
        w0, w1, w2 = w[0:1], w[1:2], w[2:3]
        y_ref[...] = (cb * (w0 * pltpu.roll(z, 2, axis=0) + w1 * pltpu.roll(z, 1, axis=0) + w2 * z)).astype(BF16)
        hb = BF16_SUBLANES
        first = (i % tiles_per_seq) == 0
        hz_prev = hc_ref[...].astype(F32) * hx_ref[...].astype(F32)
        hz_meta = mc_ref[...].astype(F32) * mx_ref[...].astype(F32)
        hz = jnp.where(first, hz_meta, hz_prev)
        zm1 = hz[hb - 1:hb]
        zm2 = hz[hb - 2:hb - 1]
        zh = z[0:hb]
        row = lax.broadcasted_iota(jnp.int32, zh.shape, 0)
        z1 = jnp.where(row == 0, zm1, pltpu.roll(zh, 1, axis=0))
        z2 = jnp.where(row == 0, zm2, jnp.where(row == 1, zm1, pltpu.roll(zh, 2, axis=0)))
        y_ref[0:hb, :] = (cb[0:hb] * (w0 * z2 + w1 * z1 + w2 * zh)).astype(BF16)

    pa = _dot(o_ref[...], wa_ref[...])
    pc = _dot(y_ref[...], wc_ref[...])
    out = _sigmoid(ga_ref[...].astype(F32)) * pa + _sigmoid(gc_ref[...].astype(F32)) * pc
    out_ref[...] = out.astype(BF16)


def _merge(o_n, u, u_meta, w_conv, wa_bf, wc_bf, *, seq, d, tm, tn):
    rows = o_n.shape[0]
    cd = d // 2
    hb = BF16_SUBLANES
    kern = functools.partial(_merge_kernel, tiles_per_seq=seq // tm)
    halo = lambda c: pl.BlockSpec((hb, cd), lambda i, j: (jnp.maximum(i * (tm // hb) - 1, 0), c))
    return pl.pallas_call(
        kern,
        grid=(rows // tm, d // tn),
        in_specs=[
            pl.BlockSpec((tm, cd), lambda i, j: (i, 0)),
            pl.BlockSpec((tm, cd), lambda i, j: (i, 3)),
            pl.BlockSpec((tm, cd), lambda i, j: (i, 4)),
            pl.BlockSpec((tm, cd), lambda i, j: (i, 5)),
            halo(3),
            halo(4),
            pl.BlockSpec((hb, cd), lambda i, j: (0, 3)),
            pl.BlockSpec((hb, cd), lambda i, j: (0, 4)),
            pl.BlockSpec((3, cd), lambda i, j: (0, 0)),
            pl.BlockSpec((tm, tn), lambda i, j: (i, 3 * d // tn + j)),
            pl.BlockSpec((tm, tn), lambda i, j: (i, 4 * d // tn + j)),
            pl.BlockSpec((cd, tn), lambda i, j: (0, j)),
            pl.BlockSpec((cd, tn), lambda i, j: (0, j)),
        ],
        out_specs=pl.BlockSpec((tm, tn), lambda i, j: (i, j)),
        out_shape=jax.ShapeDtypeStruct((rows, d), BF16),
        scratch_shapes=[pltpu.VMEM((tm, cd), BF16)],
        compiler_params=_params("arbitrary", "arbitrary"),
        name="merge",
    )(o_n, u, u, u, u, u, u_meta, u_meta, w_conv, u, u, wa_bf, wc_bf)


def _out_router_kernel(mg_ref, h_ref, wo_ref, g_ref, b_ref, wr_ref, rb_ref, tri_ref,
                       h1_ref, h1s_ref, eidx_ref, pos_ref, wtk_ref, counts_ref, cnt_ref, stage_ref, *, alpha):
    @pl.when(pl.program_id(0) == 0)
    def _():
        cnt_ref[...] = jnp.zeros_like(cnt_ref)

    m = _dot(mg_ref[...], wo_ref[...])
    h1 = _layer_norm(alpha * h_ref[...] + m, g_ref[...], b_ref[...])
    h1_ref[...] = h1
    _store_slabs(h1s_ref, h1, stage_ref)

    logits = _dot_nt(wr_ref[...], h1, precision=lax.Precision.HIGHEST)
    s = _sigmoid(logits)
    sel = s + rb_ref[...]
    n_exp, tm = sel.shape
    per_group = n_exp // N_GROUPS
    neg = -jnp.inf

    grow = lax.broadcasted_iota(jnp.int32, (per_group, tm), 0).astype(F32)
    scores = []
    for g in range(N_GROUPS):
        sg = sel[g * per_group:(g + 1) * per_group]
        m1 = jnp.max(sg, axis=0, keepdims=True)
        first = jnp.min(jnp.where(sg == m1, grow, float(per_group)), axis=0, keepdims=True)
        m2 = jnp.max(jnp.where(grow == first, neg, sg), axis=0, keepdims=True)
        scores.append(m1 + m2)
    gs = jnp.concatenate(scores, axis=0)

    gidx = lax.broadcasted_iota(jnp.int32, gs.shape, 0)
    grank = jnp.zeros(gs.shape, F32)
    for g in range(N_GROUPS):
        o = gs[g:g + 1]
        grank = grank + jnp.where((o > gs) | ((o == gs) & (gidx > g)), 1.0, 0.0)
    gkeep = jnp.where(grank < TOPK_GROUPS, 1.0, 0.0)
    keep = jnp.concatenate([jnp.broadcast_to(gkeep[g:g + 1], (per_group, tm)) for g in range(N_GROUPS)], axis=0)
    selm = jnp.where(keep > 0.5, sel, neg)

    eidx = lax.broadcasted_iota(jnp.int32, selm.shape, 0)
    rank = jnp.zeros(selm.shape, F32)
    for e in range(n_exp):
        o = selm[e:e + 1]
        rank = rank + jnp.where((o > selm) | ((o == selm) & (eidx > e)), 1.0, 0.0)
    chosen = rank < TOP_K
    ssel = jnp.where(chosen, s, 0.0)
    w = ssel / jnp.sum(ssel, axis=0, keepdims=True) * ROUTED_SCALE

    chosen_f = jnp.where(chosen, 1.0, 0.0)
    before = cnt_ref[...]
    pos = _dot(chosen_f.astype(BF16), tri_ref[...]) + before
    cnt_ref[...] = before + jnp.sum(chosen_f, axis=1, keepdims=True)
    counts_ref[...] = cnt_ref[...]

    def pick(v):
        rows = [jnp.sum(jnp.where(rank == float(r), v, 0.0), axis=0, keepdims=True) for r in range(TOP_K)]
        return jnp.concatenate(rows, axis=0)

    eidx_ref[...] = pick(eidx.astype(F32)).astype(jnp.int32)
    pos_ref[...] = pick(pos).astype(jnp.int32)
    wtk_ref[...] = pick(w)


def _out_router(merged, h, wo_bf, g, b, w_router_t, router_bias, *, tm, alpha):
    rows, d = merged.shape
    n_exp = w_router_t.shape[0]
    n_slab = d // LANES
    tri = jnp.triu(jnp.ones((tm, tm), BF16), 1)
    kern = functools.partial(_out_router_kernel, alpha=alpha)
    per_tok = lambda dt: jax.ShapeDtypeStruct((TOP_K, rows), dt)
    return pl.pallas_call(
        kern,
        grid=(rows // tm,),
        in_specs=[
            pl.BlockSpec((tm, d), lambda i: (i, 0)),
            pl.BlockSpec((tm, d), lambda i: (i, 0)),
            pl.BlockSpec((d, d), lambda i: (0, 0)),
            pl.BlockSpec((1, d), lambda i: (0, 0)),
            pl.BlockSpec((1, d), lambda i: (0, 0)),
            pl.BlockSpec((n_exp, d), lambda i: (0, 0)),
            pl.BlockSpec((n_exp, 1), lambda i: (0, 0)),
            pl.BlockSpec((tm, tm), lambda i: (0, 0)),
        ],
        out_specs=[
            pl.BlockSpec((tm, d), lambda i: (i, 0)),
            pl.BlockSpec((tm * n_slab, LANES), lambda i: (i, 0)),
            pl.BlockSpec((TOP_K, tm), lambda i: (0, i)),
            pl.BlockSpec((TOP_K, tm), lambda i: (0, i)),
            pl.BlockSpec((TOP_K, tm), lambda i: (0, i)),
            pl.BlockSpec((n_exp, 1), lambda i: (0, 0)),
        ],
        out_shape=[
            jax.ShapeDtypeStruct((rows, d), F32),
            jax.ShapeDtypeStruct((rows * n_slab, LANES), F32),
            per_tok(jnp.int32),
            per_tok(jnp.int32),
            per_tok(F32),
            jax.ShapeDtypeStruct((n_exp, 1), F32),
        ],
        scratch_shapes=[pltpu.VMEM((n_exp, 1), F32), pltpu.VMEM((tm * n_slab, LANES), F32)],
        compiler_params=_params("arbitrary"),
        name="out_router",
    )(merged, h, wo_bf, g, b, w_router_t, router_bias, tri)


def _expert_kernel(be_ref, nv_ref, xs_ref, rw_ref, wg_ref, wu_ref, wd_ref, *rest):
    ys_ref, wg_bf, wu_bf, wd_bf, slot_ref, stage_out = rest[-6:]
    s = pl.program_id(0)
    last = pl.num_programs(0) - 2
    cur = be_ref[jnp.maximum(s - 1, 0)]
    nxt = be_ref[jnp.minimum(s, last)]

    @pl.when(s == 0)
    def _():
        slot_ref[0] = 0

    slot = slot_ref[0]
    fill = jnp.where(s == 0, 0, 1 - slot)

    @pl.when((s == 0) | (nxt != cur))
    def _():
        wg_bf[fill] = wg_ref[...].astype(BF16)
        wu_bf[fill] = wu_ref[...].astype(BF16)
        wd_bf[fill] = wd_ref[...].astype(BF16)

    block = s - 1

    @pl.when((block >= 0) & (block < nv_ref[0]))
    def _():
        d, d_exp = wg_ref.shape
        x = _slab_rows(xs_ref, xs_ref.shape[0] * LANES // d, d).astype(BF16)
        g = _dot(x, wg_bf[slot])
        u = _dot(x, wu_bf[slot])
        hid = g * _sigmoid(g) * u
        _store_slabs(ys_ref, _dot(hid.astype(BF16), wd_bf[slot]) * rw_ref[...], stage_out)

    @pl.when(block >= nv_ref[0])
    def _():
        ys_ref[...] = jnp.zeros_like(ys_ref)

    @pl.when((s > 0) & (nxt != cur))
    def _():
        slot_ref[0] = 1 - slot


def _experts(block_e, n_valid, xs, row_w, w_gate, w_up, w_down, *, bm, block0, total_blocks, out_prev):
    _, d, d_exp = w_gate.shape
    n_slab = d // LANES
    nb = xs.shape[0] // (bm * n_slab)
    ahead = lambda s, be, nv: (be[jnp.minimum(s, nb - 1)], 0, 0)
    grid_spec = pltpu.PrefetchScalarGridSpec(
        num_scalar_prefetch=2,
        grid=(nb + 1,),
        in_specs=[
            pl.BlockSpec((bm * n_slab, LANES), lambda s, be, nv: (jnp.maximum(jnp.minimum(s - 1, nv[0] - 1), 0), 0)),
            pl.BlockSpec((bm, 1), lambda s, be, nv: (jnp.maximum(jnp.minimum(s - 1, nv[0] - 1), 0), 0)),
            pl.BlockSpec((None, d, d_exp), ahead),
            pl.BlockSpec((None, d, d_exp), ahead),
            pl.BlockSpec((None, d_exp, d), ahead),
        ] + ([pl.BlockSpec(memory_space=pl.ANY)] if out_prev is not None else []),
        out_specs=pl.BlockSpec((bm * n_slab, LANES), lambda s, be, nv: (block0 + jnp.maximum(s - 1, 0), 0)),
        scratch_shapes=[
            pltpu.VMEM((2, d, d_exp), BF16),
            pltpu.VMEM((2, d, d_exp), BF16),
            pltpu.VMEM((2, d_exp, d), BF16),
            pltpu.SMEM((1,), jnp.int32),
            pltpu.VMEM((bm * n_slab, LANES), F32),
        ],
    )
    return pl.pallas_call(
        _expert_kernel,
        grid_spec=grid_spec,
        out_shape=jax.ShapeDtypeStruct((total_blocks * bm * n_slab, LANES), BF16),
        input_output_aliases={7: 0} if out_prev is not None else {},
        compiler_params=_params("arbitrary"),
        name="experts",
    )(block_e, n_valid, xs, row_w, w_gate, w_up, w_down, *([out_prev] if out_prev is not None else []))


def _shared_kernel(h1_ref, wg_ref, wu_ref, wd_ref, o_ref):
    x = h1_ref[...].astype(BF16)
    gate = _dot(x, wg_ref[...])
    up = _dot(x, wu_ref[...])
    hid = gate * _sigmoid(gate) * up
    o_ref[...] = _dot(hid.astype(BF16), wd_ref[...]).astype(BF16)


def _shared(h1, wg_bf, wu_bf, wd_bf, *, tm):
    rows, d = h1.shape
    d_exp = wg_bf.shape[1]
    return pl.pallas_call(
        _shared_kernel,
        grid=(rows // tm,),
        in_specs=[
            pl.BlockSpec((tm, d), lambda i: (i, 0)),
            pl.BlockSpec((d, d_exp), lambda i: (0, 0)),
            pl.BlockSpec((d, d_exp), lambda i: (0, 0)),
            pl.BlockSpec((d_exp, d), lambda i: (0, 0)),
        ],
        out_specs=pl.BlockSpec((tm, d), lambda i: (i, 0)),
        out_shape=jax.ShapeDtypeStruct((rows, d), BF16),
        compiler_params=_params("arbitrary"),
        name="shared",
    )(h1, wg_bf, wu_bf, wd_bf)


def _final_kernel(*refs, alpha):
    h1_ref, sh_ref = refs[:2]
    y_refs = refs[2:2 + TOP_K]
    g_ref, b_ref = refs[2 + TOP_K:4 + TOP_K]
    out_ref, stage_ref = refs[-2:]
    h1 = h1_ref[...]
    tm, d = h1.shape
    acc = y_refs[0][...].astype(F32)
    for r in range(1, TOP_K):
        acc = acc + y_refs[r][...].astype(F32)
    stage_ref[...] = acc
    f = sh_ref[...].astype(F32) + _slab_rows(stage_ref, tm, d)
    out_ref[...] = _layer_norm(alpha * h1 + f, g_ref[...], b_ref[...])


def _final(h1, shared, yg, g, b, *, tm, alpha, row0, out_prev):
    total_rows, d = h1.shape
    n_slab = d // LANES
    tiles = yg.shape[0] // (TOP_K * tm * n_slab)
    first = row0 // tm
    kern = functools.partial(_final_kernel, alpha=alpha)
    y_spec = lambda r: pl.BlockSpec((tm * n_slab, LANES), lambda i: (r * tiles + i, 0))
    args = [h1, shared, *([yg] * TOP_K), g, b] + ([out_prev] if out_prev is not None else [])
    return pl.pallas_call(
        kern,
        grid=(tiles,),
        in_specs=[pl.BlockSpec((tm, d), lambda i: (first + i, 0)), pl.BlockSpec((tm, d), lambda i: (first + i, 0))]
        + [y_spec(r) for r in range(TOP_K)]
        + [pl.BlockSpec((1, d), lambda i: (0, 0)), pl.BlockSpec((1, d), lambda i: (0, 0))]
        + ([pl.BlockSpec(memory_space=pl.ANY)] if out_prev is not None else []),
        out_specs=pl.BlockSpec((tm, d), lambda i: (first + i, 0)),
        out_shape=jax.ShapeDtypeStruct((total_rows, d), F32),
        input_output_aliases={len(args) - 1: 0} if out_prev is not None else {},
        scratch_shapes=[pltpu.VMEM((tm * n_slab, LANES), F32)],
        compiler_params=_params("arbitrary"),
        name="final",
    )(*args)


def _rope_tables(pos, dh):
    half = dh // 2
    inv_freq = ROPE_THETA ** (-jnp.arange(half, dtype=F32) / half)
    ang = pos.astype(F32)[:, None] * inv_freq[None, :]
    cos = jnp.cos(ang)
    sin = jnp.sin(ang)
    reps = LANES // dh
    cos_full = jnp.tile(jnp.concatenate([cos, cos], axis=1), (1, reps))
    sin_full = jnp.tile(jnp.concatenate([-sin, sin], axis=1), (1, reps))
    return cos_full, sin_full


def _tile(n, pref):
    t = min(n, pref)
    assert n % t == 0, (n, pref)
    return t


def kernel(x, meta_tokens, ln0_g, ln0_b, w_in, lambda_q1, lambda_k1, lambda_q2, lambda_k2, subln_g, w_conv, w_proj_attn, w_proj_conv, w_out, ln1_g, ln1_b, w_router, router_bias, w_exp_gate, w_exp_up, w_exp_down, w_sh_gate, w_sh_up, w_sh_down, ln2_g, ln2_b):
    batch, seq, d = x.shape
    n_meta = meta_tokens.shape[0]
    depth = w_in.shape[0]
    dh = lambda_q1.shape[-1]
    n_exp = w_router.shape[-1]
    assert depth == 1 and 2 * dh == LANES and n_meta == BF16_SUBLANES
    assert n_exp // N_GROUPS == 8 and subln_g.shape[-1] == 2 * dh
    cd = d // 2
    heads = cd // (2 * dh)
    n_tok = batch * seq
    alpha = float((2 * depth) ** 0.25)

    row = lambda a: a.reshape(1, -1).astype(F32)

    w_in_bf = w_in[0].astype(BF16)
    cos_m, sin_m = _rope_tables(jnp.arange(n_meta), dh)
    cos_r, sin_r = _rope_tables(jnp.arange(n_meta, n_meta + seq), dh)
    tn_in = _tile(cd, 1024)
    tm_in = _tile(seq, 512)
    inproj = functools.partial(_in_proj, g=row(ln0_g), b=row(ln0_b), w_bf=w_in_bf, tn=tn_in, qk_width=cd, dh=dh)
    u, h = inproj(x.reshape(n_tok, d), cos=cos_r, sin=sin_r, tm=tm_in)
    u_meta, _ = inproj(meta_tokens.astype(F32), cos=cos_m, sin=sin_m, tm=n_meta)

    lamv = jnp.stack([lambda_q1[0], lambda_k1[0], lambda_q2[0], lambda_k2[0]]).astype(F32)
    o_n = _attention(u, u_meta, lamv, row(subln_g[0]), batch=batch, seq=seq, heads=heads, dh=dh,
                     tq=_tile(seq, 512))

    merged = _merge(o_n, u, u_meta, w_conv[0].astype(F32), w_proj_attn[0].astype(BF16),
                    w_proj_conv[0].astype(BF16), seq=seq, d=d, tm=_tile(seq, 512), tn=_tile(d, 1024))

    h1, h1s, eidx, pos_tk, w_tk, counts = _out_router(
        merged, h, w_out[0].astype(BF16), row(ln1_g[0]), row(ln1_b[0]),
        w_router[0].T.astype(F32), router_bias[0].reshape(n_exp, 1).astype(F32),
        tm=_tile(seq, 256), alpha=alpha)

    bm = 256
    n_slab = d // LANES
    n_asg = n_tok * TOP_K
    nb = -(-n_asg // bm) + n_exp
    counts = counts[:, 0].astype(jnp.int32)
    ends = jnp.cumsum(counts)
    starts = ends - counts
    padded = (counts + bm - 1) // bm * bm
    pad_end = jnp.cumsum(padded)
    pad_start = pad_end - padded
    experts = jnp.arange(n_exp, dtype=jnp.int32)
    onehot = eidx[..., None] == experts
    lookup = lambda table: jnp.sum(jnp.where(onehot, table, 0), axis=-1)
    dense_tk = pos_tk + lookup(starts)
    dest_tk = pos_tk + lookup(pad_start)
    tok = jnp.broadcast_to(jnp.arange(n_tok, dtype=jnp.int32)[None], dense_tk.shape)
    _, sorted_tok, sorted_w = lax.sort((dense_tk.reshape(-1), tok.reshape(-1), w_tk.reshape(-1)), num_keys=1)
    block_row = jnp.arange(nb, dtype=jnp.int32) * bm
    block_e = jnp.minimum(jnp.sum(pad_end[None, :] <= block_row[:, None], axis=1), n_exp - 1).astype(jnp.int32)
    n_valid = (pad_end[-1:] // bm).astype(jnp.int32)
    block_hot = block_e[:, None] == experts
    block_lookup = lambda table: jnp.sum(jnp.where(block_hot, table, 0), axis=-1)
    in_expert = (block_row - block_lookup(pad_start))[:, None] + jnp.arange(bm, dtype=jnp.int32)
    dense_row = jnp.minimum(block_lookup(starts)[:, None] + in_expert, n_asg - 1)
    real = in_expert < block_lookup(counts)[:, None]
    spread = (block_row[:, None] + jnp.arange(bm, dtype=jnp.int32)) % n_tok
    row_tok = jnp.where(real, sorted_tok.at[dense_row].get(mode="promise_in_bounds"), spread).reshape(-1)
    row_w = jnp.where(real, sorted_w.at[dense_row].get(mode="promise_in_bounds"), 0.0).reshape(-1, 1)

    shared = _shared(h1, w_sh_gate[0].astype(BF16), w_sh_up[0].astype(BF16), w_sh_down[0].astype(BF16),
                     tm=_tile(seq, 512))

    h1_slabs = h1s.reshape(n_tok, n_slab, LANES)
    chunks = EXPERT_CHUNKS if nb % EXPERT_CHUNKS == 0 else 1
    cb = nb // chunks
    ys = None
    for c in range(chunks):
        lo, hi = c * cb * bm, (c + 1) * cb * bm
        xs = h1_slabs.at[row_tok[lo:hi]].get(mode="promise_in_bounds")
        ys = _experts(block_e[c * cb:(c + 1) * cb], jnp.clip(n_valid - c * cb, 0, cb),
                      xs.reshape(cb * bm * n_slab, LANES), row_w[lo:hi],
                      w_exp_gate[0], w_exp_up[0], w_exp_down[0], bm=bm, block0=c * cb, total_blocks=nb, out_prev=ys)

    y_slabs = ys.reshape(nb * bm, n_slab, LANES)
    tm_f = _tile(seq, 256)
    chunks = COMBINE_CHUNKS if n_tok % (COMBINE_CHUNKS * tm_f) == 0 else 1
    ct = n_tok // chunks
    out = None
    for c in range(chunks):
        yg = y_slabs.at[dest_tk[:, c * ct:(c + 1) * ct].reshape(-1)].get(mode="promise_in_bounds")
        out = _final(h1, shared, yg.reshape(TOP_K * ct * n_slab, LANES), row(ln2_g[0]), row(ln2_b[0]),
                     tm=tm_f, alpha=alpha, row0=c * ct, out_prev=out)
    return out.reshape(batch, seq, d)
```

```python
import functools

import jax
import jax.numpy as jnp
import numpy as np
from jax import lax
from jax.experimental import pallas as pl
from jax.experimental.pallas import tpu as pltpu

CHUNK = 64
ROPE_THETA = 10000.0
LN_EPS = 1e-5
TOP_K = 8
N_GROUPS = 8
TOPK_GROUPS = 4
ROUTED_SCALE = 2.5
LAM_INIT = 0.2
LOG2E = 1.4426950408889634
EXPERT_CHUNKS = 4
COMBINE_CHUNKS = 2

LANES = 128
BF16_SUBLANES = 16
VMEM_LIMIT_BYTES = 56 * 1024 * 1024

F32 = jnp.float32
BF16 = jnp.bfloat16


def _params(*sem):
    return pltpu.CompilerParams(dimension_semantics=sem, vmem_limit_bytes=VMEM_LIMIT_BYTES)


def _layer_norm(x, g, b):
    mu = jnp.mean(x, axis=-1, keepdims=True)
    xc = x - mu
    var = jnp.mean(xc * xc, axis=-1, keepdims=True)
    return xc * lax.rsqrt(var + LN_EPS) * g + b


def _dot(a, b):
    return jnp.dot(a, b, preferred_element_type=F32)


def _dot_nt(a, b, **kw):
    return lax.dot_general(a, b, (((1,), (1,)), ((), ())), preferred_element_type=F32, **kw)


def _sigmoid(x):
    return 1.0 / (1.0 + jnp.exp(-x))


def _store_slabs(ref, x, stage):
    rows, d = x.shape
    n_slab = d // LANES
    dst = ref if ref.dtype == F32 else stage
    for c in range(n_slab):
        dst[pl.ds(c, rows, stride=n_slab), :] = x[:, c * LANES:(c + 1) * LANES]
    if dst is not ref:
        ref[...] = stage[...].astype(ref.dtype)


def _slab_rows(ref, rows, d):
    n_slab = d // LANES
    return jnp.concatenate([ref[pl.ds(c, rows, stride=n_slab), :] for c in range(n_slab)], axis=1)


def _inproj_kernel(x_ref, g_ref, b_ref, w_ref, cos_ref, sin_ref, u_ref, h_ref, hb_ref,
                   *, q_tiles, qk_scale, half):
    j = pl.program_id(1)

    @pl.when(j == 0)
    def _():
        h = _layer_norm(x_ref[...], g_ref[...], b_ref[...])
        h_ref[...] = h
        hb_ref[...] = h.astype(BF16)

    acc = _dot(hb_ref[...], w_ref[...])
    tn = acc.shape[1]

    @pl.when(j < 2 * q_tiles)
    def _():
        cos = cos_ref[...]
        sin = sin_ref[...]
        lane = lax.broadcasted_iota(jnp.int32, cos.shape, 1)
        first = (lane % (2 * half)) < half
        scale = jnp.where(j < q_tiles, qk_scale, 1.0).astype(F32)
        for c in range(tn // LANES):
            a = acc[:, c * LANES:(c + 1) * LANES]
            partner = jnp.where(first, pltpu.roll(a, LANES - half, axis=1), pltpu.roll(a, half, axis=1))
            u_ref[:, c * LANES:(c + 1) * LANES] = ((a * cos + partner * sin) * scale).astype(BF16)

    @pl.when(j >= 2 * q_tiles)
    def _():
        u_ref[...] = acc.astype(BF16)


def _in_proj(x2d, g, b, w_bf, cos, sin, *, tm, tn, qk_width, dh):
    rows, d = x2d.shape
    cols = w_bf.shape[1]
    pos_blocks = cos.shape[0] // tm
    kern = functools.partial(_inproj_kernel, q_tiles=qk_width // tn, qk_scale=float(dh) ** -0.5 * LOG2E, half=dh // 2)
    return pl.pallas_call(
        kern,
        grid=(rows // tm, cols // tn),
        in_specs=[
            pl.BlockSpec((tm, d), lambda i, j: (i, 0)),
            pl.BlockSpec((1, d), lambda i, j: (0, 0)),
            pl.BlockSpec((1, d), lambda i, j: (0, 0)),
            pl.BlockSpec((d, tn), lambda i, j: (0, j)),
            pl.BlockSpec((tm, LANES), lambda i, j: (i % pos_blocks, 0)),
            pl.BlockSpec((tm, LANES), lambda i, j: (i % pos_blocks, 0)),
        ],
        out_specs=[
            pl.BlockSpec((tm, tn), lambda i, j: (i, j)),
            pl.BlockSpec((tm, d), lambda i, j: (i, 0)),
        ],
        out_shape=[
            jax.ShapeDtypeStruct((rows, cols), BF16),
            jax.ShapeDtypeStruct((rows, d), F32),
        ],
        scratch_shapes=[pltpu.VMEM((tm, d), BF16)],
        compiler_params=_params("arbitrary", "arbitrary"),
        name="in_proj",
    )(x2d, g, b, w_bf, cos, sin)


def _attn_kernel(q_ref, k_ref, v_ref, km_ref, vm_ref, lamv_ref, g_ref, o_ref, vt_ref, vmt_ref, s_ref, *, tq, dh):
    qi = pl.program_id(2)
    seq = k_ref.shape[0]

    @pl.when(qi == 0)
    def _():
        for c in range(seq // tq):
            vt_ref[:, c * tq:(c + 1) * tq] = v_ref[c * tq:(c + 1) * tq, :].astype(F32).T.astype(BF16)
        vmt_ref[...] = vm_ref[...].astype(F32).T.astype(BF16)

    qt = q_ref[...].astype(F32).T
    dim = lax.broadcasted_iota(jnp.int32, qt.shape, 0)
    qq = jnp.concatenate([jnp.where(dim < dh, qt, 0.0), jnp.where(dim >= dh, qt, 0.0)], axis=1).astype(BF16)

    s = _dot(km_ref[...], qq)
    m = jnp.max(s, axis=0, keepdims=True)
    p = jnp.exp2(s - m)
    l = jnp.sum(p, axis=0, keepdims=True)
    acc = _dot(vmt_ref[...], p.astype(BF16))

    def update(carry, s, vt):
        m, l, acc = carry
        m_new = jnp.maximum(m, jnp.max(s, axis=0, keepdims=True))
        a = jnp.exp2(m - m_new)
        p = jnp.exp2(s - m_new)
        l = a * l + jnp.sum(p, axis=0, keepdims=True)
        acc = a * acc + _dot(vt, p.astype(BF16))
        return m_new, l, acc

    def scores(kb):
        return _dot(k_ref[pl.ds(pl.multiple_of(kb * tq, tq), tq), :], qq)

    def values(kb):
        return vt_ref[:, pl.ds(pl.multiple_of(kb * tq, tq), tq)]

    def block_pair(i, carry):
        s_ref[1] = scores(2 * i + 1)
        carry = update(carry, s_ref[0], values(2 * i))
        s_ref[0] = scores(2 * i + 2)
        return update(carry, s_ref[1], values(2 * i + 1))

    def odd_block(carry):
        s_ref[1] = scores(qi)
        carry = update(carry, s_ref[0], values(qi - 1))
        s_ref[0] = s_ref[1]
        return carry

    s_ref[0] = scores(0)
    carry = lax.fori_loop(0, qi // 2, block_pair, (m, l, acc))
    m, l, acc = lax.cond(qi % 2 == 1, odd_block, lambda c: c, carry)
    s = s_ref[0]

    start = pl.multiple_of(qi * tq, tq)
    key = lax.broadcasted_iota(jnp.int32, s.shape, 0)
    qry = lax.broadcasted_iota(jnp.int32, s.shape, 1)
    qry = jnp.where(qry >= tq, qry - tq, qry)
    s = jnp.where((key // CHUNK) <= (qry // CHUNK), s, -jnp.inf)
    m, l, acc = update((m, l, acc), s, vt_ref[:, pl.ds(start, tq)])

    lamv = lamv_ref[...]
    lam = (jnp.exp(jnp.sum(lamv[0:1] * lamv[1:2], axis=1, keepdims=True))
           - jnp.exp(jnp.sum(lamv[2:3] * lamv[3:4], axis=1, keepdims=True)) + LAM_INIT)
    o_all = acc / l
    o = (o_all[:, :tq] - lam * o_all[:, tq:]).T
    ms = jnp.mean(o * o, axis=-1, keepdims=True)
    o = o * lax.rsqrt(ms + LN_EPS) * g_ref[...] * (1.0 - LAM_INIT)
    o_ref[...] = o.astype(BF16)


def _attention(u, u_meta, lamv, subln_g, *, batch, seq, heads, dh, tq):
    vd = 2 * dh
    nq = seq // tq
    kern = functools.partial(_attn_kernel, tq=tq, dh=dh)
    n_meta = u_meta.shape[0]
    return pl.pallas_call(
        kern,
        grid=(batch, heads, nq),
        in_specs=[
            pl.BlockSpec((tq, vd), lambda b, h, i: (b * nq + i, h)),
            pl.BlockSpec((seq, vd), lambda b, h, i: (b, heads + h)),
            pl.BlockSpec((seq, vd), lambda b, h, i: (b, 2 * heads + h)),
            pl.BlockSpec((n_meta, vd), lambda b, h, i: (0, heads + h)),
            pl.BlockSpec((n_meta, vd), lambda b, h, i: (0, 2 * heads + h)),
            pl.BlockSpec((4, dh), lambda b, h, i: (0, 0)),
            pl.BlockSpec((1, vd), lambda b, h, i: (0, 0)),
        ],
        out_specs=pl.BlockSpec((tq, vd), lambda b, h, i: (b * nq + i, h)),
        out_shape=jax.ShapeDtypeStruct((batch * seq, heads * vd), BF16),
        scratch_shapes=[pltpu.VMEM((vd, seq), BF16), pltpu.VMEM((vd, n_meta), BF16),
                        pltpu.VMEM((2, tq, 2 * tq), F32)],
        compiler_params=_params("arbitrary", "arbitrary", "arbitrary"),
        name="attention",
    )(u, u, u, u_meta, u_meta, lamv, subln_g)


def _merge_kernel(o_ref, cx_ref, cc_ref, cb_ref, hx_ref, hc_ref, mx_ref, mc_ref, wconv_ref,
                  ga_ref, gc_ref, wa_ref, wc_ref, out_ref, y_ref, *, tiles_per_seq):
    i = pl.program_id(0)
    j = pl.program_id(1)

    @pl.when(j == 0)
    def _():
        z = cc_ref[...].astype(F32) * cx_ref[...].astype(F32)
        cb = cb_ref[...].astype(F32)
        w = wconv_ref[...]
        w0, w1, w2 = w[0:1], w[1:2], w[2:3]
        y_ref[...] = (cb * (w0 * pltpu.roll(z, 2, axis=0) + w1 * pltpu.roll(z, 1, axis=0) + w2 * z)).astype(BF16)
        hb = BF16_SUBLANES
        first = (i % tiles_per_seq) == 0
        hz_prev = hc_ref[...].astype(F32) * hx_ref[...].astype(F32)
        hz_meta = mc_ref[...].astype(F32) * mx_ref[...].astype(F32)
        hz = jnp.where(first, hz_meta, hz_prev)
        zm1 = hz[hb - 1:hb]
        zm2 = hz[hb - 2:hb - 1]
        zh = z[0:hb]
        row = lax.broadcasted_iota(jnp.int32, zh.shape, 0)
        z1 = jnp.where(row == 0, zm1, pltpu.roll(zh, 1, axis=0))
        z2 = jnp.where(row == 0, zm2, jnp.where(row == 1, zm1, pltpu.roll(zh, 2, axis=0)))
        y_ref[0:hb, :] = (cb[0:hb] * (w0 * z2 + w1 * z1 + w2 * zh)).astype(BF16)

    pa = _dot(o_ref[...], wa_ref[...])
    pc = _dot(y_ref[...], wc_ref[...])
    out = _sigmoid(ga_ref[...].astype(F32)) * pa + _sigmoid(gc_ref[...].astype(F32)) * pc
    out_ref[...] = out.astype(BF16)


def _merge(o_n, u, u_meta, w_conv, wa_bf, wc_bf, *, seq, d, tm, tn):
    rows = o_n.shape[0]
    cd = d // 2
    hb = BF16_SUBLANES
    kern = functools.partial(_merge_kernel, tiles_per_seq=seq // tm)
    halo = lambda c: pl.BlockSpec((hb, cd), lambda i, j: (jnp.maximum(i * (tm // hb) - 1, 0), c))
    return pl.pallas_call(
        kern,
        grid=(rows // tm, d // tn),
        in_specs=[
            pl.BlockSpec((tm, cd), lambda i, j: (i, 0)),
            pl.BlockSpec((tm, cd), lambda i, j: (i, 3)),
            pl.BlockSpec((tm, cd), lambda i, j: (i, 4)),
            pl.BlockSpec((tm, cd), lambda i, j: (i, 5)),
            halo(3),
            halo(4),
            pl.BlockSpec((hb, cd), lambda i, j: (0, 3)),
            pl.BlockSpec((hb, cd), lambda i, j: (0, 4)),
            pl.BlockSpec((3, cd), lambda i, j: (0, 0)),
            pl.BlockSpec((tm, tn), lambda i, j: (i, 3 * d // tn + j)),
            pl.BlockSpec((tm, tn), lambda i, j: (i, 4 * d // tn + j)),
            pl.BlockSpec((cd, tn), lambda i, j: (0, j)),
            pl.BlockSpec((cd, tn), lambda i, j: (0, j)),
        ],
        out_specs=pl.BlockSpec((tm, tn), lambda i, j: (i, j)),
        out_shape=jax.ShapeDtypeStruct((rows, d), BF16),
        scratch_shapes=[pltpu.VMEM((tm, cd), BF16)],
        compiler_params=_params("arbitrary", "arbitrary"),
        name="merge",
    )(o_n, u, u, u, u, u, u_meta, u_meta, w_conv, u, u, wa_bf, wc_bf)


def _out_router_kernel(mg_ref, h_ref, wo_ref, g_ref, b_ref, wr_ref, rb_ref, tri_ref,
                       h1_ref, h1s_ref, eidx_ref, pos_ref, wtk_ref, counts_ref, cnt_ref, stage_ref, prev_ref, *, alpha):
    step = pl.program_id(0)

    @pl.when(step == 0)
    def _():
        cnt_ref[...] = jnp.zeros_like(cnt_ref)
        prev_ref[...] = jnp.zeros_like(prev_ref)

    logits = _dot_nt(wr_ref[...], prev_ref[...], precision=lax.Precision.HIGHEST)
    s = _sigmoid(logits)
    sel = s + rb_ref[...]
    n_exp, tm = sel.shape
    per_group = n_exp // N_GROUPS
    neg = -jnp.inf

    grow = lax.broadcasted_iota(jnp.int32, (per_group, tm), 0).astype(F32)
    scores = []
    for g in range(N_GROUPS):
        sg = sel[g * per_group:(g + 1) * per_group]
        m1 = jnp.max(sg, axis=0, keepdims=True)
        first = jnp.min(jnp.where(sg == m1, grow, float(per_group)), axis=0, keepdims=True)
        m2 = jnp.max(jnp.where(grow == first, neg, sg), axis=0, keepdims=True)
        scores.append(m1 + m2)
    gs = jnp.concatenate(scores, axis=0)

    gidx = lax.broadcasted_iota(jnp.int32, gs.shape, 0)
    grank = jnp.zeros(gs.shape, F32)
    for g in range(N_GROUPS):
        o = gs[g:g + 1]
        grank = grank + jnp.where((o > gs) | ((o == gs) & (gidx > g)), 1.0, 0.0)
    gkeep = jnp.where(grank < TOPK_GROUPS, 1.0, 0.0)
    keep = jnp.concatenate([jnp.broadcast_to(gkeep[g:g + 1], (per_group, tm)) for g in range(N_GROUPS)], axis=0)
    selm = jnp.where(keep > 0.5, sel, neg)

    eidx = lax.broadcasted_iota(jnp.int32, selm.shape, 0)
    rank = jnp.zeros(selm.shape, F32)
    for e in range(n_exp):
        o = selm[e:e + 1]
        rank = rank + jnp.where((o > selm) | ((o == selm) & (eidx > e)), 1.0, 0.0)
    chosen = rank < TOP_K
    ssel = jnp.where(chosen, s, 0.0)
    w = ssel / jnp.sum(ssel, axis=0, keepdims=True) * ROUTED_SCALE

    chosen_f = jnp.where(chosen & (step > 0), 1.0, 0.0)
    before = cnt_ref[...]
    pos = _dot(chosen_f.astype(BF16), tri_ref[...]) + before
    cnt_ref[...] = before + jnp.sum(chosen_f, axis=1, keepdims=True)
    counts_ref[...] = cnt_ref[...]

    def pick(v):
        rows = [jnp.sum(jnp.where(rank == float(r), v, 0.0), axis=0, keepdims=True) for r in range(TOP_K)]
        return jnp.concatenate(rows, axis=0)

    eidx_ref[...] = pick(eidx.astype(F32)).astype(jnp.int32)
    pos_ref[...] = pick(pos).astype(jnp.int32)
    wtk_ref[...] = pick(w)

    m = _dot(mg_ref[...], wo_ref[...])
    h1 = _layer_norm(alpha * h_ref[...] + m, g_ref[...], b_ref[...])
    h1_ref[...] = h1
    _store_slabs(h1s_ref, h1, stage_ref)
    prev_ref[...] = h1


def _out_router(merged, h, wo_bf, g, b, w_router_t, router_bias, *, tm, alpha):
    rows, d = merged.shape
    n_exp = w_router_t.shape[0]
    n_slab = d // LANES
    tiles = rows // tm
    tri = jnp.triu(jnp.ones((tm, tm), BF16), 1)
    kern = functools.partial(_out_router_kernel, alpha=alpha)
    per_tok = lambda dt: jax.ShapeDtypeStruct((TOP_K, rows), dt)
    return pl.pallas_call(
        kern,
        grid=(tiles + 1,),
        in_specs=[
            pl.BlockSpec((tm, d), lambda i: (jnp.minimum(i, tiles - 1), 0)),
            pl.BlockSpec((tm, d), lambda i: (jnp.minimum(i, tiles - 1), 0)),
            pl.BlockSpec((d, d), lambda i: (0, 0)),
            pl.BlockSpec((1, d), lambda i: (0, 0)),
            pl.BlockSpec((1, d), lambda i: (0, 0)),
            pl.BlockSpec((n_exp, d), lambda i: (0, 0)),
            pl.BlockSpec((n_exp, 1), lambda i: (0, 0)),
            pl.BlockSpec((tm, tm), lambda i: (0, 0)),
        ],
        out_specs=[
            pl.BlockSpec((tm, d), lambda i: (jnp.minimum(i, tiles - 1), 0)),
            pl.BlockSpec((tm * n_slab, LANES), lambda i: (jnp.minimum(i, tiles - 1), 0)),
            pl.BlockSpec((TOP_K, tm), lambda i: (0, jnp.maximum(i - 1, 0))),
            pl.BlockSpec((TOP_K, tm), lambda i: (0, jnp.maximum(i - 1, 0))),
            pl.BlockSpec((TOP_K, tm), lambda i: (0, jnp.maximum(i - 1, 0))),
            pl.BlockSpec((n_exp, 1), lambda i: (0, 0)),
        ],
        out_shape=[
            jax.ShapeDtypeStruct((rows, d), F32),
            jax.ShapeDtypeStruct((rows * n_slab, LANES), F32),
            per_tok(jnp.int32),
            per_tok(jnp.int32),
            per_tok(F32),
            jax.ShapeDtypeStruct((n_exp, 1), F32),
        ],
        scratch_shapes=[pltpu.VMEM((n_exp, 1), F32), pltpu.VMEM((tm * n_slab, LANES), F32),
                        pltpu.VMEM((tm, d), F32)],
        compiler_params=_params("arbitrary"),
        name="out_router",
    )(merged, h, wo_bf, g, b, w_router_t, router_bias, tri)


def _expert_kernel(be_ref, nv_ref, xs_ref, rw_ref, wg_ref, wu_ref, wd_ref, *rest):
    ys_ref, wg_bf, wu_bf, wd_bf, slot_ref, stage_out = rest[-6:]
    s = pl.program_id(0)
    last = pl.num_programs(0) - 2
    cur = be_ref[jnp.maximum(s - 1, 0)]
    nxt = be_ref[jnp.minimum(s, last)]

    @pl.when(s == 0)
    def _():
        slot_ref[0] = 0

    slot = slot_ref[0]
    fill = jnp.where(s == 0, 0, 1 - slot)

    @pl.when((s == 0) | (nxt != cur))
    def _():
        wg_bf[fill] = wg_ref[...].astype(BF16)
        wu_bf[fill] = wu_ref[...].astype(BF16)
        wd_bf[fill] = wd_ref[...].astype(BF16)

    block = s - 1

    @pl.when((block >= 0) & (block < nv_ref[0]))
    def _():
        d, d_exp = wg_ref.shape
        x = _slab_rows(xs_ref, xs_ref.shape[0] * LANES // d, d).astype(BF16)
        g = _dot(x, wg_bf[slot])
        u = _dot(x, wu_bf[slot])
        hid = g * _sigmoid(g) * u
        _store_slabs(ys_ref, _dot(hid.astype(BF16), wd_bf[slot]) * rw_ref[...], stage_out)

    @pl.when(block >= nv_ref[0])
    def _():
        ys_ref[...] = jnp.zeros_like(ys_ref)

    @pl.when((s > 0) & (nxt != cur))
    def _():
        slot_ref[0] = 1 - slot


def _experts(block_e, n_valid, xs, row_w, w_gate, w_up, w_down, *, bm, block0, total_blocks, out_prev):
    _, d, d_exp = w_gate.shape
    n_slab = d // LANES
    nb = xs.shape[0] // (bm * n_slab)
    ahead = lambda s, be, nv: (be[jnp.minimum(s, nb - 1)], 0, 0)
    grid_spec = pltpu.PrefetchScalarGridSpec(
        num_scalar_prefetch=2,
        grid=(nb + 1,),
        in_specs=[
            pl.BlockSpec((bm * n_slab, LANES), lambda s, be, nv: (jnp.maximum(jnp.minimum(s - 1, nv[0] - 1), 0), 0)),
            pl.BlockSpec((bm, 1), lambda s, be, nv: (jnp.maximum(jnp.minimum(s - 1, nv[0] - 1), 0), 0)),
            pl.BlockSpec((None, d, d_exp), ahead),
            pl.BlockSpec((None, d, d_exp), ahead),
            pl.BlockSpec((None, d_exp, d), ahead),
        ] + ([pl.BlockSpec(memory_space=pl.ANY)] if out_prev is not None else []),
        out_specs=pl.BlockSpec((bm * n_slab, LANES), lambda s, be, nv: (block0 + jnp.maximum(s - 1, 0), 0)),
        scratch_shapes=[
            pltpu.VMEM((2, d, d_exp), BF16),
            pltpu.VMEM((2, d, d_exp), BF16),
            pltpu.VMEM((2, d_exp, d), BF16),
            pltpu.SMEM((1,), jnp.int32),
            pltpu.VMEM((bm * n_slab, LANES), F32),
        ],
    )
    return pl.pallas_call(
        _expert_kernel,
        grid_spec=grid_spec,
        out_shape=jax.ShapeDtypeStruct((total_blocks * bm * n_slab, LANES), BF16),
        input_output_aliases={7: 0} if out_prev is not None else {},
        compiler_params=_params("arbitrary"),
        name="experts",
    )(block_e, n_valid, xs, row_w, w_gate, w_up, w_down, *([out_prev] if out_prev is not None else []))


def _shared_kernel(h1_ref, wg_ref, wu_ref, wd_ref, o_ref):
    x = h1_ref[...].astype(BF16)
    gate = _dot(x, wg_ref[...])
    up = _dot(x, wu_ref[...])
    hid = gate * _sigmoid(gate) * up
    o_ref[...] = _dot(hid.astype(BF16), wd_ref[...]).astype(BF16)


def _shared(h1, wg_bf, wu_bf, wd_bf, *, tm):
    rows, d = h1.shape
    d_exp = wg_bf.shape[1]
    return pl.pallas_call(
        _shared_kernel,
        grid=(rows // tm,),
        in_specs=[
            pl.BlockSpec((tm, d), lambda i: (i, 0)),
            pl.BlockSpec((d, d_exp), lambda i: (0, 0)),
            pl.BlockSpec((d, d_exp), lambda i: (0, 0)),
            pl.BlockSpec((d_exp, d), lambda i: (0, 0)),
        ],
        out_specs=pl.BlockSpec((tm, d), lambda i: (i, 0)),
        out_shape=jax.ShapeDtypeStruct((rows, d), BF16),
        compiler_params=_params("arbitrary"),
        name="shared",
    )(h1, wg_bf, wu_bf, wd_bf)


def _final_kernel(*refs, alpha):
    h1_ref, sh_ref = refs[:2]
    y_refs = refs[2:2 + TOP_K]
    g_ref, b_ref = refs[2 + TOP_K:4 + TOP_K]
    out_ref, stage_ref = refs[-2:]
    h1 = h1_ref[...]
    tm, d = h1.shape
    acc = y_refs[0][...].astype(F32)
    for r in range(1, TOP_K):
        acc = acc + y_refs[r][...].astype(F32)
    stage_ref[...] = acc
    f = sh_ref[...].astype(F32) + _slab_rows(stage_ref, tm, d)
    out_ref[...] = _layer_norm(alpha * h1 + f, g_ref[...], b_ref[...])


def _final(h1, shared, yg, g, b, *, tm, alpha, row0, out_prev):
    total_rows, d = h1.shape
    n_slab = d // LANES
    tiles = yg.shape[0] // (TOP_K * tm * n_slab)
    first = row0 // tm
    kern = functools.partial(_final_kernel, alpha=alpha)
    y_spec = lambda r: pl.BlockSpec((tm * n_slab, LANES), lambda i: (r * tiles + i, 0))
    args = [h1, shared, *([yg] * TOP_K), g, b] + ([out_prev] if out_prev is not None else [])
    return pl.pallas_call(
        kern,
        grid=(tiles,),
        in_specs=[pl.BlockSpec((tm, d), lambda i: (first + i, 0)), pl.BlockSpec((tm, d), lambda i: (first + i, 0))]
        + [y_spec(r) for r in range(TOP_K)]
        + [pl.BlockSpec((1, d), lambda i: (0, 0)), pl.BlockSpec((1, d), lambda i: (0, 0))]
        + ([pl.BlockSpec(memory_space=pl.ANY)] if out_prev is not None else []),
        out_specs=pl.BlockSpec((tm, d), lambda i: (first + i, 0)),
        out_shape=jax.ShapeDtypeStruct((total_rows, d), F32),
        input_output_aliases={len(args) - 1: 0} if out_prev is not None else {},
        scratch_shapes=[pltpu.VMEM((tm * n_slab, LANES), F32)],
        compiler_params=_params("arbitrary"),
        name="final",
    )(*args)


def _rope_tables(pos, dh):
    half = dh // 2
    inv_freq = ROPE_THETA ** (-jnp.arange(half, dtype=F32) / half)
    ang = pos.astype(F32)[:, None] * inv_freq[None, :]
    cos = jnp.cos(ang)
    sin = jnp.sin(ang)
    reps = LANES // dh
    cos_full = jnp.tile(jnp.concatenate([cos, cos], axis=1), (1, reps))
    sin_full = jnp.tile(jnp.concatenate([-sin, sin], axis=1), (1, reps))
    return cos_full, sin_full


def _tile(n, pref):
    t = min(n, pref)
    assert n % t == 0, (n, pref)
    return t


def kernel(x, meta_tokens, ln0_g, ln0_b, w_in, lambda_q1, lambda_k1, lambda_q2, lambda_k2, subln_g, w_conv, w_proj_attn, w_proj_conv, w_out, ln1_g, ln1_b, w_router, router_bias, w_exp_gate, w_exp_up, w_exp_down, w_sh_gate, w_sh_up, w_sh_down, ln2_g, ln2_b):
    batch, seq, d = x.shape
    n_meta = meta_tokens.shape[0]
    depth = w_in.shape[0]
    dh = lambda_q1.shape[-1]
    n_exp = w_router.shape[-1]
    assert depth == 1 and 2 * dh == LANES and n_meta == BF16_SUBLANES
    assert n_exp // N_GROUPS == 8 and subln_g.shape[-1] == 2 * dh
    cd = d // 2
    heads = cd // (2 * dh)
    n_tok = batch * seq
    alpha = float((2 * depth) ** 0.25)

    row = lambda a: a.reshape(1, -1).astype(F32)

    w_in_bf = w_in[0].astype(BF16)
    cos_m, sin_m = _rope_tables(jnp.arange(n_meta), dh)
    cos_r, sin_r = _rope_tables(jnp.arange(n_meta, n_meta + seq), dh)
    tn_in = _tile(cd, 1024)
    tm_in = _tile(seq, 512)
    inproj = functools.partial(_in_proj, g=row(ln0_g), b=row(ln0_b), w_bf=w_in_bf, tn=tn_in, qk_width=cd, dh=dh)
    u, h = inproj(x.reshape(n_tok, d), cos=cos_r, sin=sin_r, tm=tm_in)
    u_meta, _ = inproj(meta_tokens.astype(F32), cos=cos_m, sin=sin_m, tm=n_meta)

    lamv = jnp.stack([lambda_q1[0], lambda_k1[0], lambda_q2[0], lambda_k2[0]]).astype(F32)
    o_n = _attention(u, u_meta, lamv, row(subln_g[0]), batch=batch, seq=seq, heads=heads, dh=dh,
                     tq=_tile(seq, 512))

    merged = _merge(o_n, u, u_meta, w_conv[0].astype(F32), w_proj_attn[0].astype(BF16),
                    w_proj_conv[0].astype(BF16), seq=seq, d=d, tm=_tile(seq, 512), tn=_tile(d, 1024))

    h1, h1s, eidx, pos_tk, w_tk, counts = _out_router(
        merged, h, w_out[0].astype(BF16), row(ln1_g[0]), row(ln1_b[0]),
        w_router[0].T.astype(F32), router_bias[0].reshape(n_exp, 1).astype(F32),
        tm=_tile(seq, 256), alpha=alpha)

    bm = 256
    n_slab = d // LANES
    n_asg = n_tok * TOP_K
    nb = -(-n_asg // bm) + n_exp
    counts = counts[:, 0].astype(jnp.int32)
    ends = jnp.cumsum(counts)
    starts = ends - counts
    padded = (counts + bm - 1) // bm * bm
    pad_end = jnp.cumsum(padded)
    pad_start = pad_end - padded
    experts = jnp.arange(n_exp, dtype=jnp.int32)
    onehot = eidx[..., None] == experts
    lookup = lambda table: jnp.sum(jnp.where(onehot, table, 0), axis=-1)
    dense_tk = pos_tk + lookup(starts)
    dest_tk = pos_tk + lookup(pad_start)
    tok = jnp.broadcast_to(jnp.arange(n_tok, dtype=jnp.int32)[None], dense_tk.shape)
    _, sorted_tok, sorted_w = lax.sort((dense_tk.reshape(-1), tok.reshape(-1), w_tk.reshape(-1)), num_keys=1)
    block_row = jnp.arange(nb, dtype=jnp.int32) * bm
    block_e = jnp.minimum(jnp.sum(pad_end[None, :] <= block_row[:, None], axis=1), n_exp - 1).astype(jnp.int32)
    n_valid = (pad_end[-1:] // bm).astype(jnp.int32)
    block_hot = block_e[:, None] == experts
    block_lookup = lambda table: jnp.sum(jnp.where(block_hot, table, 0), axis=-1)
    in_expert = (block_row - block_lookup(pad_start))[:, None] + jnp.arange(bm, dtype=jnp.int32)
    dense_row = jnp.minimum(block_lookup(starts)[:, None] + in_expert, n_asg - 1)
    real = in_expert < block_lookup(counts)[:, None]
    spread = (block_row[:, None] + jnp.arange(bm, dtype=jnp.int32)) % n_tok
    row_tok = jnp.where(real, sorted_tok.at[dense_row].get(mode="promise_in_bounds"), spread).reshape(-1)
    row_w = jnp.where(real, sorted_w.at[dense_row].get(mode="promise_in_bounds"), 0.0).reshape(-1, 1)

    shared = _shared(h1, w_sh_gate[0].astype(BF16), w_sh_up[0].astype(BF16), w_sh_down[0].astype(BF16),
                     tm=_tile(seq, 512))

    h1_slabs = h1s.reshape(n_tok, n_slab, LANES)
    chunks = EXPERT_CHUNKS if nb % EXPERT_CHUNKS == 0 else 1
    cb = nb // chunks
    ys = None
    for c in range(chunks):
        lo, hi = c * cb * bm, (c + 1) * cb * bm
        xs = h1_slabs.at[row_tok[lo:hi]].get(mode="promise_in_bounds")
        ys = _experts(block_e[c * cb:(c + 1) * cb], jnp.clip(n_valid - c * cb, 0, cb),
                      xs.reshape(cb * bm * n_slab, LANES), row_w[lo:hi],
                      w_exp_gate[0], w_exp_up[0], w_exp_down[0], bm=bm, block0=c * cb, total_blocks=nb, out_prev=ys)

    y_slabs = ys.reshape(nb * bm, n_slab, LANES)
    tm_f = _tile(seq, 256)
    chunks = COMBINE_CHUNKS if n_tok % (COMBINE_CHUNKS * tm_f) == 0 else 1
    ct = n_tok // chunks
    out = None
    for c in range(chunks):
        yg = y_slabs.at[dest_tk[:, c * ct:(c + 1) * ct].reshape(-1)].get(mode="promise_in_bounds")
        out = _final(h1, shared, yg.reshape(TOP_K * ct * n_slab, LANES), row(ln2_g[0]), row(ln2_b[0]),
                     tm=tm_f, alpha=alpha, row0=c * ct, out_prev=out)
    return out.reshape(batch, seq, d)
```

```python
import functools

import jax
import jax.numpy as jnp
from jax import lax
from jax.experimental import pallas as pl
from jax.experimental.pallas import tpu as pltpu

CHUNK = 64
ROPE_THETA = 10000.0
LN_EPS = 1e-5
TOP_K = 8
N_GROUPS = 8
TOPK_GROUPS = 4
ROUTED_SCALE = 2.5
LAM_INIT = 0.2
LOG2E = 1.4426950408889634

LANES = 128
BF16_SUBLANES = 16
VMEM_LIMIT_BYTES = 56 * 1024 * 1024

F32 = jnp.float32
BF16 = jnp.bfloat16


def _params(*sem):
    return pltpu.CompilerParams(dimension_semantics=sem, vmem_limit_bytes=VMEM_LIMIT_BYTES)


def _layer_norm(x, g, b):
    mu = jnp.mean(x, axis=-1, keepdims=True)
    xc = x - mu
    var = jnp.mean(xc * xc, axis=-1, keepdims=True)
    return xc * lax.rsqrt(var + LN_EPS) * g + b


def _dot(a, b):
    return jnp.dot(a, b, preferred_element_type=F32)


def _dot_nt(a, b, **kw):
    return lax.dot_general(a, b, (((1,), (1,)), ((), ())), preferred_element_type=F32, **kw)


def _sigmoid(x):
    return 1.0 / (1.0 + jnp.exp(-x))


def _store_slabs(ref, x, stage):
    rows, d = x.shape
    n_slab = d // LANES
    dst = ref if ref.dtype == F32 else stage
    for c in range(n_slab):
        dst[pl.ds(c, rows, stride=n_slab), :] = x[:, c * LANES:(c + 1) * LANES]
    if dst is not ref:
        ref[...] = stage[...].astype(ref.dtype)


def _slab_rows(ref, rows, d):
    n_slab = d // LANES
    return jnp.concatenate([ref[pl.ds(c, rows, stride=n_slab), :] for c in range(n_slab)], axis=1)


def _inproj_kernel(x_ref, g_ref, b_ref, w_ref, cos_ref, sin_ref, u_ref, h_ref, hb_ref,
                   *, q_tiles, qk_scale, half):
    j = pl.program_id(1)

    @pl.when(j == 0)
    def _():
        h = _layer_norm(x_ref[...], g_ref[...], b_ref[...])
        h_ref[...] = h
        hb_ref[...] = h.astype(BF16)

    acc = _dot(hb_ref[...], w_ref[...])
    tn = acc.shape[1]

    @pl.when(j < 2 * q_tiles)
    def _():
        cos = cos_ref[...]
        sin = sin_ref[...]
        lane = lax.broadcasted_iota(jnp.int32, cos.shape, 1)
        first = (lane % (2 * half)) < half
        scale = jnp.where(j < q_tiles, qk_scale, 1.0).astype(F32)
        for c in range(tn // LANES):
            a = acc[:, c * LANES:(c + 1) * LANES]
            partner = jnp.where(first, pltpu.roll(a, LANES - half, axis=1), pltpu.roll(a, half, axis=1))
            u_ref[:, c * LANES:(c + 1) * LANES] = ((a * cos + partner * sin) * scale).astype(BF16)

    @pl.when(j >= 2 * q_tiles)
    def _():
        u_ref[...] = acc.astype(BF16)


def _in_proj(x2d, g, b, w_bf, cos, sin, *, tm, tn, qk_width, dh):
    rows, d = x2d.shape
    cols = w_bf.shape[1]
    pos_blocks = cos.shape[0] // tm
    kern = functools.partial(_inproj_kernel, q_tiles=qk_width // tn, qk_scale=float(dh) ** -0.5 * LOG2E, half=dh // 2)
    return pl.pallas_call(
        kern,
        grid=(rows // tm, cols // tn),
        in_specs=[
            pl.BlockSpec((tm, d), lambda i, j: (i, 0)),
            pl.BlockSpec((1, d), lambda i, j: (0, 0)),
            pl.BlockSpec((1, d), lambda i, j: (0, 0)),
            pl.BlockSpec((d, tn), lambda i, j: (0, j)),
            pl.BlockSpec((tm, LANES), lambda i, j: (i % pos_blocks, 0)),
            pl.BlockSpec((tm, LANES), lambda i, j: (i % pos_blocks, 0)),
        ],
        out_specs=[
            pl.BlockSpec((tm, tn), lambda i, j: (i, j)),
            pl.BlockSpec((tm, d), lambda i, j: (i, 0)),
        ],
        out_shape=[
            jax.ShapeDtypeStruct((rows, cols), BF16),
            jax.ShapeDtypeStruct((rows, d), F32),
        ],
        scratch_shapes=[pltpu.VMEM((tm, d), BF16)],
        compiler_params=_params("arbitrary", "arbitrary"),
        name="in_proj",
    )(x2d, g, b, w_bf, cos, sin)


def _attn_kernel(q_ref, k_ref, v_ref, km_ref, vm_ref, lamv_ref, g_ref, o_ref, vt_ref, vmt_ref, s_ref, *, tq, dh):
    qi = pl.program_id(2)
    seq = k_ref.shape[0]

    @pl.when(qi == 0)
    def _():
        for c in range(seq // tq):
            vt_ref[:, c * tq:(c + 1) * tq] = v_ref[c * tq:(c + 1) * tq, :].astype(F32).T.astype(BF16)
        vmt_ref[...] = vm_ref[...].astype(F32).T.astype(BF16)

    qt = q_ref[...].astype(F32).T
    dim = lax.broadcasted_iota(jnp.int32, qt.shape, 0)
    qq = jnp.concatenate([jnp.where(dim < dh, qt, 0.0), jnp.where(dim >= dh, qt, 0.0)], axis=1).astype(BF16)

    s = _dot(km_ref[...], qq)
    m = jnp.max(s, axis=0, keepdims=True)
    p = jnp.exp2(s - m)
    l = jnp.sum(p, axis=0, keepdims=True)
    acc = _dot(vmt_ref[...], p.astype(BF16))

    def update(carry, s, vt):
        m, l, acc = carry
        m_new = jnp.maximum(m, jnp.max(s, axis=0, keepdims=True))
        a = jnp.exp2(m - m_new)
        p = jnp.exp2(s - m_new)
        l = a * l + jnp.sum(p, axis=0, keepdims=True)
        acc = a * acc + _dot(vt, p.astype(BF16))
        return m_new, l, acc

    def scores(kb):
        return _dot(k_ref[pl.ds(pl.multiple_of(kb * tq, tq), tq), :], qq)

    def values(kb):
        return vt_ref[:, pl.ds(pl.multiple_of(kb * tq, tq), tq)]

    def block_pair(i, carry):
        s_ref[1] = scores(2 * i + 1)
        carry = update(carry, s_ref[0], values(2 * i))
        s_ref[0] = scores(2 * i + 2)
        return update(carry, s_ref[1], values(2 * i + 1))

    def odd_block(carry):
        s_ref[1] = scores(qi)
        carry = update(carry, s_ref[0], values(qi - 1))
        s_ref[0] = s_ref[1]
        return carry

    s_ref[0] = scores(0)
    carry = lax.fori_loop(0, qi // 2, block_pair, (m, l, acc))
    m, l, acc = lax.cond(qi % 2 == 1, odd_block, lambda c: c, carry)
    s = s_ref[0]

    start = pl.multiple_of(qi * tq, tq)
    key = lax.broadcasted_iota(jnp.int32, s.shape, 0)
    qry = lax.broadcasted_iota(jnp.int32, s.shape, 1)
    qry = jnp.where(qry >= tq, qry - tq, qry)
    s = jnp.where((key // CHUNK) <= (qry // CHUNK), s, -jnp.inf)
    m, l, acc = update((m, l, acc), s, vt_ref[:, pl.ds(start, tq)])

    lamv = lamv_ref[...]
    lam = (jnp.exp(jnp.sum(lamv[0:1] * lamv[1:2], axis=1, keepdims=True))
           - jnp.exp(jnp.sum(lamv[2:3] * lamv[3:4], axis=1, keepdims=True)) + LAM_INIT)
    o_all = acc / l
    o = (o_all[:, :tq] - lam * o_all[:, tq:]).T
    ms = jnp.mean(o * o, axis=-1, keepdims=True)
    o = o * lax.rsqrt(ms + LN_EPS) * g_ref[...] * (1.0 - LAM_INIT)
    o_ref[...] = o.astype(BF16)


def _attention(u, u_meta, lamv, subln_g, *, batch, seq, heads, dh, tq):
    vd = 2 * dh
    nq = seq // tq
    kern = functools.partial(_attn_kernel, tq=tq, dh=dh)
    n_meta = u_meta.shape[0]
    return pl.pallas_call(
        kern,
        grid=(batch, heads, nq),
        in_specs=[
            pl.BlockSpec((tq, vd), lambda b, h, i: (b * nq + i, h)),
            pl.BlockSpec((seq, vd), lambda b, h, i: (b, heads + h)),
            pl.BlockSpec((seq, vd), lambda b, h, i: (b, 2 * heads + h)),
            pl.BlockSpec((n_meta, vd), lambda b, h, i: (0, heads + h)),
            pl.BlockSpec((n_meta, vd), lambda b, h, i: (0, 2 * heads + h)),
            pl.BlockSpec((4, dh), lambda b, h, i: (0, 0)),
            pl.BlockSpec((1, vd), lambda b, h, i: (0, 0)),
        ],
        out_specs=pl.BlockSpec((tq, vd), lambda b, h, i: (b * nq + i, h)),
        out_shape=jax.ShapeDtypeStruct((batch * seq, heads * vd), BF16),
        scratch_shapes=[pltpu.VMEM((vd, seq), BF16), pltpu.VMEM((vd, n_meta), BF16),
                        pltpu.VMEM((2, tq, 2 * tq), F32)],
        compiler_params=_params("arbitrary", "arbitrary", "arbitrary"),
        name="attention",
    )(u, u, u, u_meta, u_meta, lamv, subln_g)


def _merge_kernel(o_ref, cx_ref, cc_ref, cb_ref, hx_ref, hc_ref, mx_ref, mc_ref, wconv_ref,
                  ga_ref, gc_ref, wa_ref, wc_ref, out_ref, y_ref, *, tiles_per_seq):
    i = pl.program_id(0)
    j = pl.program_id(1)

    @pl.when(j == 0)
    def _():
        z = cc_ref[...].astype(F32) * cx_ref[...].astype(F32)
        cb = cb_ref[...].astype(F32)
        w = wconv_ref[...]
        w0, w1, w2 = w[0:1], w[1:2], w[2:3]
        y_ref[...] = (cb * (w0 * pltpu.roll(z, 2, axis=0) + w1 * pltpu.roll(z, 1, axis=0) + w2 * z)).astype(BF16)
        hb = BF16_SUBLANES
        first = (i % tiles_per_seq) == 0
        hz_prev = hc_ref[...].astype(F32) * hx_ref[...].astype(F32)
        hz_meta = mc_ref[...].astype(F32) * mx_ref[...].astype(F32)
        hz = jnp.where(first, hz_meta, hz_prev)
        zm1 = hz[hb - 1:hb]
        zm2 = hz[hb - 2:hb - 1]
        zh = z[0:hb]
        row = lax.broadcasted_iota(jnp.int32, zh.shape, 0)
        z1 = jnp.where(row == 0, zm1, pltpu.roll(zh, 1, axis=0))
        z2 = jnp.where(row == 0, zm2, jnp.where(row == 1, zm1, pltpu.roll(zh, 2, axis=0)))
        y_ref[0:hb, :] = (cb[0:hb] * (w0 * z2 + w1 * z1 + w2 * zh)).astype(BF16)

    pa = _dot(o_ref[...], wa_ref[...])
    pc = _dot(y_ref[...], wc_ref[...])
    out = _sigmoid(ga_ref[...].astype(F32)) * pa + _sigmoid(gc_ref[...].astype(F32)) * pc
    out_ref[...] = out.astype(BF16)


def _merge(o_n, u, u_meta, w_conv, wa_bf, wc_bf, *, seq, d, tm, tn):
    rows = o_n.shape[0]
    cd = d // 2
    hb = BF16_SUBLANES
    kern = functools.partial(_merge_kernel, tiles_per_seq=seq // tm)
    halo = lambda c: pl.BlockSpec((hb, cd), lambda i, j: (jnp.maximum(i * (tm // hb) - 1, 0), c))
    return pl.pallas_call(
        kern,
        grid=(rows // tm, d // tn),
        in_specs=[
            pl.BlockSpec((tm, cd), lambda i, j: (i, 0)),
            pl.BlockSpec((tm, cd), lambda i, j: (i, 3)),
            pl.BlockSpec((tm, cd), lambda i, j: (i, 4)),
            pl.BlockSpec((tm, cd), lambda i, j: (i, 5)),
            halo(3),
            halo(4),
            pl.BlockSpec((hb, cd), lambda i, j: (0, 3)),
            pl.BlockSpec((hb, cd), lambda i, j: (0, 4)),
            pl.BlockSpec((3, cd), lambda i, j: (0, 0)),
            pl.BlockSpec((tm, tn), lambda i, j: (i, 3 * d // tn + j)),
            pl.BlockSpec((tm, tn), lambda i, j: (i, 4 * d // tn + j)),
            pl.BlockSpec((cd, tn), lambda i, j: (0, j)),
            pl.BlockSpec((cd, tn), lambda i, j: (0, j)),
        ],
        out_specs=pl.BlockSpec((tm, tn), lambda i, j: (i, j)),
        out_shape=jax.ShapeDtypeStruct((rows, d), BF16),
        scratch_shapes=[pltpu.VMEM((tm, cd), BF16)],
        compiler_params=_params("arbitrary", "arbitrary"),
        name="merge",
    )(o_n, u, u, u, u, u, u_meta, u_meta, w_conv, u, u, wa_bf, wc_bf)


def _out_router_kernel(mg_ref, h_ref, wo_ref, g_ref, b_ref, wr_ref, rb_ref, tri_ref,
                       h1_ref, h1s_ref, eidx_ref, pos_ref, wtk_ref, counts_ref, cnt_ref, *, alpha):
    @pl.when(pl.program_id(0) == 0)
    def _():
        cnt_ref[...] = jnp.zeros_like(cnt_ref)

    m = _dot(mg_ref[...], wo_ref[...])
    h1 = _layer_norm(alpha * h_ref[...] + m, g_ref[...], b_ref[...])
    h1_ref[...] = h1
    _store_slabs(h1s_ref, h1, None)

    logits = _dot_nt(wr_ref[...], h1, precision=lax.Precision.HIGHEST)
    s = _sigmoid(logits)
    sel = s + rb_ref[...]
    n_exp, tm = sel.shape
    per_group = n_exp // N_GROUPS
    neg = -jnp.inf

    grow = lax.broadcasted_iota(jnp.int32, (per_group, tm), 0).astype(F32)
    scores = []
    for g in range(N_GROUPS):
        sg = sel[g * per_group:(g + 1) * per_group]
        m1 = jnp.max(sg, axis=0, keepdims=True)
        first = jnp.min(jnp.where(sg == m1, grow, float(per_group)), axis=0, keepdims=True)
        m2 = jnp.max(jnp.where(grow == first, neg, sg), axis=0, keepdims=True)
        scores.append(m1 + m2)
    gs = jnp.concatenate(scores, axis=0)

    gidx = lax.broadcasted_iota(jnp.int32, gs.shape, 0)
    grank = jnp.zeros(gs.shape, F32)
    for g in range(N_GROUPS):
        o = gs[g:g + 1]
        grank = grank + jnp.where((o > gs) | ((o == gs) & (gidx > g)), 1.0, 0.0)
    gkeep = jnp.where(grank < TOPK_GROUPS, 1.0, 0.0)
    keep = jnp.concatenate([jnp.broadcast_to(gkeep[g:g + 1], (per_group, tm)) for g in range(N_GROUPS)], axis=0)
    selm = jnp.where(keep > 0.5, sel, neg)

    eidx = lax.broadcasted_iota(jnp.int32, selm.shape, 0)
    rank = jnp.zeros(selm.shape, F32)
    for e in range(n_exp):
        o = selm[e:e + 1]
        rank = rank + jnp.where((o > selm) | ((o == selm) & (eidx > e)), 1.0, 0.0)
    chosen = rank < TOP_K
    ssel = jnp.where(chosen, s, 0.0)
    w = ssel / jnp.sum(ssel, axis=0, keepdims=True) * ROUTED_SCALE

    chosen_f = jnp.where(chosen, 1.0, 0.0)
    before = cnt_ref[...]
    pos = _dot(chosen_f.astype(BF16), tri_ref[...]) + before
    cnt_ref[...] = before + jnp.sum(chosen_f, axis=1, keepdims=True)
    counts_ref[...] = cnt_ref[...]

    def pick(v):
        rows = [jnp.sum(jnp.where(rank == float(r), v, 0.0), axis=0, keepdims=True) for r in range(TOP_K)]
        return jnp.concatenate(rows, axis=0)

    eidx_ref[...] = pick(eidx.astype(F32)).astype(jnp.int32)
    pos_ref[...] = pick(pos).astype(jnp.int32)
    wtk_ref[...] = pick(w)


def _out_router(merged, h, wo_bf, g, b, w_router_t, router_bias, *, tm, alpha):
    rows, d = merged.shape
    n_exp = w_router_t.shape[0]
    n_slab = d // LANES
    tri = jnp.triu(jnp.ones((tm, tm), BF16), 1)
    kern = functools.partial(_out_router_kernel, alpha=alpha)
    per_tok = lambda dt: jax.ShapeDtypeStruct((TOP_K, rows), dt)
    return pl.pallas_call(
        kern,
        grid=(rows // tm,),
        in_specs=[
            pl.BlockSpec((tm, d), lambda i: (i, 0)),
            pl.BlockSpec((tm, d), lambda i: (i, 0)),
            pl.BlockSpec((d, d), lambda i: (0, 0)),
            pl.BlockSpec((1, d), lambda i: (0, 0)),
            pl.BlockSpec((1, d), lambda i: (0, 0)),
            pl.BlockSpec((n_exp, d), lambda i: (0, 0)),
            pl.BlockSpec((n_exp, 1), lambda i: (0, 0)),
            pl.BlockSpec((tm, tm), lambda i: (0, 0)),
        ],
        out_specs=[
            pl.BlockSpec((tm, d), lambda i: (i, 0)),
            pl.BlockSpec((tm * n_slab, LANES), lambda i: (i, 0)),
            pl.BlockSpec((TOP_K, tm), lambda i: (0, i)),
            pl.BlockSpec((TOP_K, tm), lambda i: (0, i)),
            pl.BlockSpec((TOP_K, tm), lambda i: (0, i)),
            pl.BlockSpec((n_exp, 1), lambda i: (0, 0)),
        ],
        out_shape=[
            jax.ShapeDtypeStruct((rows, d), F32),
            jax.ShapeDtypeStruct((rows * n_slab, LANES), F32),
            per_tok(jnp.int32),
            per_tok(jnp.int32),
            per_tok(F32),
            jax.ShapeDtypeStruct((n_exp, 1), F32),
        ],
        scratch_shapes=[pltpu.VMEM((n_exp, 1), F32)],
        compiler_params=_params("arbitrary"),
        name="out_router",
    )(merged, h, wo_bf, g, b, w_router_t, router_bias, tri)


def _expert_kernel(be_ref, nv_ref, xs_ref, rw_ref, wg_ref, wu_ref, wd_ref, ys_ref, wg_bf, wu_bf, wd_bf, slot_ref,
                   stage_out):
    s = pl.program_id(0)
    last = pl.num_programs(0) - 2
    cur = be_ref[jnp.maximum(s - 1, 0)]
    nxt = be_ref[jnp.minimum(s, last)]

    @pl.when(s == 0)
    def _():
        slot_ref[0] = 0

    slot = slot_ref[0]
    fill = jnp.where(s == 0, 0, 1 - slot)

    @pl.when((s == 0) | (nxt != cur))
    def _():
        wg_bf[fill] = wg_ref[...].astype(BF16)
        wu_bf[fill] = wu_ref[...].astype(BF16)
        wd_bf[fill] = wd_ref[...].astype(BF16)

    block = s - 1

    @pl.when((block >= 0) & (block < nv_ref[0]))
    def _():
        d, d_exp = wg_ref.shape
        x = _slab_rows(xs_ref, xs_ref.shape[0] * LANES // d, d).astype(BF16)
        g = _dot(x, wg_bf[slot])
        u = _dot(x, wu_bf[slot])
        hid = g * _sigmoid(g) * u
        _store_slabs(ys_ref, _dot(hid.astype(BF16), wd_bf[slot]) * rw_ref[...], stage_out)

    @pl.when(block >= nv_ref[0])
    def _():
        ys_ref[...] = jnp.zeros_like(ys_ref)

    @pl.when((s > 0) & (nxt != cur))
    def _():
        slot_ref[0] = 1 - slot


def _experts(block_e, n_valid, xs, row_w, w_gate, w_up, w_down, *, bm):
    _, d, d_exp = w_gate.shape
    n_slab = d // LANES
    nb = xs.shape[0] // (bm * n_slab)
    ahead = lambda s, be, nv: (be[jnp.minimum(s, nb - 1)], 0, 0)
    grid_spec = pltpu.PrefetchScalarGridSpec(
        num_scalar_prefetch=2,
        grid=(nb + 1,),
        in_specs=[
            pl.BlockSpec((bm * n_slab, LANES), lambda s, be, nv: (jnp.maximum(jnp.minimum(s - 1, nv[0] - 1), 0), 0)),
            pl.BlockSpec((bm, 1), lambda s, be, nv: (jnp.maximum(jnp.minimum(s - 1, nv[0] - 1), 0), 0)),
            pl.BlockSpec((None, d, d_exp), ahead),
            pl.BlockSpec((None, d, d_exp), ahead),
            pl.BlockSpec((None, d_exp, d), ahead),
        ],
        out_specs=pl.BlockSpec((bm * n_slab, LANES), lambda s, be, nv: (jnp.maximum(s - 1, 0), 0)),
        scratch_shapes=[
            pltpu.VMEM((2, d, d_exp), BF16),
            pltpu.VMEM((2, d, d_exp), BF16),
            pltpu.VMEM((2, d_exp, d), BF16),
            pltpu.SMEM((1,), jnp.int32),
            pltpu.VMEM((bm * n_slab, LANES), F32),
        ],
    )
    return pl.pallas_call(
        _expert_kernel,
        grid_spec=grid_spec,
        out_shape=jax.ShapeDtypeStruct(xs.shape, BF16),
        compiler_params=_params("arbitrary"),
        name="experts",
    )(block_e, n_valid, xs, row_w, w_gate, w_up, w_down)


def _final_kernel(*refs, alpha):
    h1_ref = refs[0]
    y_refs = refs[1:1 + TOP_K]
    wg_ref, wu_ref, wd_ref, g_ref, b_ref, out_ref, stage_ref = refs[1 + TOP_K:]
    h1 = h1_ref[...]
    tm, d = h1.shape
    x = h1.astype(BF16)
    gate = _dot(x, wg_ref[...])
    up = _dot(x, wu_ref[...])
    hid = gate * _sigmoid(gate) * up
    shared = _dot(hid.astype(BF16), wd_ref[...])
    acc = y_refs[0][...].astype(F32)
    for r in range(1, TOP_K):
        acc = acc + y_refs[r][...].astype(F32)
    stage_ref[...] = acc
    f = shared + _slab_rows(stage_ref, tm, d)
    out_ref[...] = _layer_norm(alpha * h1 + f, g_ref[...], b_ref[...])


def _final(h1, yg, wg_bf, wu_bf, wd_bf, g, b, *, tm, alpha):
    rows, d = h1.shape
    d_exp = wg_bf.shape[1]
    n_slab = d // LANES
    tiles = rows // tm
    kern = functools.partial(_final_kernel, alpha=alpha)
    y_spec = lambda r: pl.BlockSpec((tm * n_slab, LANES), lambda i: (r * tiles + i, 0))
    return pl.pallas_call(
        kern,
        grid=(tiles,),
        in_specs=[pl.BlockSpec((tm, d), lambda i: (i, 0))] + [y_spec(r) for r in range(TOP_K)] + [
            pl.BlockSpec((d, d_exp), lambda i: (0, 0)),
            pl.BlockSpec((d, d_exp), lambda i: (0, 0)),
            pl.BlockSpec((d_exp, d), lambda i: (0, 0)),
            pl.BlockSpec((1, d), lambda i: (0, 0)),
            pl.BlockSpec((1, d), lambda i: (0, 0)),
        ],
        out_specs=pl.BlockSpec((tm, d), lambda i: (i, 0)),
        out_shape=jax.ShapeDtypeStruct((rows, d), F32),
        scratch_shapes=[pltpu.VMEM((tm * n_slab, LANES), F32)],
        compiler_params=_params("arbitrary"),
        name="final",
    )(h1, *([yg] * TOP_K), wg_bf, wu_bf, wd_bf, g, b)


def _rope_tables(pos, dh):
    half = dh // 2
    inv_freq = ROPE_THETA ** (-jnp.arange(half, dtype=F32) / half)
    ang = pos.astype(F32)[:, None] * inv_freq[None, :]
    cos = jnp.cos(ang)
    sin = jnp.sin(ang)
    reps = LANES // dh
    cos_full = jnp.tile(jnp.concatenate([cos, cos], axis=1), (1, reps))
    sin_full = jnp.tile(jnp.concatenate([-sin, sin], axis=1), (1, reps))
    return cos_full, sin_full


def _tile(n, pref):
    t = min(n, pref)
    assert n % t == 0, (n, pref)
    return t


def kernel(x, meta_tokens, ln0_g, ln0_b, w_in, lambda_q1, lambda_k1, lambda_q2, lambda_k2, subln_g, w_conv, w_proj_attn, w_proj_conv, w_out, ln1_g, ln1_b, w_router, router_bias, w_exp_gate, w_exp_up, w_exp_down, w_sh_gate, w_sh_up, w_sh_down, ln2_g, ln2_b):
    batch, seq, d = x.shape
    n_meta = meta_tokens.shape[0]
    depth = w_in.shape[0]
    dh = lambda_q1.shape[-1]
    n_exp = w_router.shape[-1]
    assert depth == 1 and 2 * dh == LANES and n_meta == BF16_SUBLANES
    assert n_exp // N_GROUPS == 8 and subln_g.shape[-1] == 2 * dh
    cd = d // 2
    heads = cd // (2 * dh)
    n_tok = batch * seq
    alpha = float((2 * depth) ** 0.25)

    row = lambda a: a.reshape(1, -1).astype(F32)

    w_in_bf = w_in[0].astype(BF16)
    cos_m, sin_m = _rope_tables(jnp.arange(n_meta), dh)
    cos_r, sin_r = _rope_tables(jnp.arange(n_meta, n_meta + seq), dh)
    tn_in = _tile(cd, 1024)
    tm_in = _tile(seq, 512)
    inproj = functools.partial(_in_proj, g=row(ln0_g), b=row(ln0_b), w_bf=w_in_bf, tn=tn_in, qk_width=cd, dh=dh)
    u, h = inproj(x.reshape(n_tok, d), cos=cos_r, sin=sin_r, tm=tm_in)
    u_meta, _ = inproj(meta_tokens.astype(F32), cos=cos_m, sin=sin_m, tm=n_meta)

    lamv = jnp.stack([lambda_q1[0], lambda_k1[0], lambda_q2[0], lambda_k2[0]]).astype(F32)
    o_n = _attention(u, u_meta, lamv, row(subln_g[0]), batch=batch, seq=seq, heads=heads, dh=dh,
                     tq=_tile(seq, 512))

    merged = _merge(o_n, u, u_meta, w_conv[0].astype(F32), w_proj_attn[0].astype(BF16),
                    w_proj_conv[0].astype(BF16), seq=seq, d=d, tm=_tile(seq, 512), tn=_tile(d, 1024))

    h1, h1s, eidx, pos_tk, w_tk, counts = _out_router(
        merged, h, w_out[0].astype(BF16), row(ln1_g[0]), row(ln1_b[0]),
        w_router[0].T.astype(F32), router_bias[0].reshape(n_exp, 1).astype(F32),
        tm=_tile(seq, 256), alpha=alpha)

    bm = 256
    n_slab = d // LANES
    n_asg = n_tok * TOP_K
    nb = -(-n_asg // bm) + n_exp
    counts = counts[:, 0].astype(jnp.int32)
    ends = jnp.cumsum(counts)
    starts = ends - counts
    padded = (counts + bm - 1) // bm * bm
    pad_end = jnp.cumsum(padded)
    pad_start = pad_end - padded
    experts = jnp.arange(n_exp, dtype=jnp.int32)
    onehot = eidx[..., None] == experts
    lookup = lambda table: jnp.sum(jnp.where(onehot, table, 0), axis=-1)
    dense_tk = pos_tk + lookup(starts)
    dest_tk = pos_tk + lookup(pad_start)
    tok = jnp.broadcast_to(jnp.arange(n_tok, dtype=jnp.int32)[None], dense_tk.shape)
    _, sorted_tok, sorted_w = lax.sort((dense_tk.reshape(-1), tok.reshape(-1), w_tk.reshape(-1)), num_keys=1)
    block_row = jnp.arange(nb, dtype=jnp.int32) * bm
    block_e = jnp.minimum(jnp.sum(pad_end[None, :] <= block_row[:, None], axis=1), n_exp - 1).astype(jnp.int32)
    n_valid = (pad_end[-1:] // bm).astype(jnp.int32)
    block_hot = block_e[:, None] == experts
    block_lookup = lambda table: jnp.sum(jnp.where(block_hot, table, 0), axis=-1)
    in_expert = (block_row - block_lookup(pad_start))[:, None] + jnp.arange(bm, dtype=jnp.int32)
    dense_row = jnp.minimum(block_lookup(starts)[:, None] + in_expert, n_asg - 1)
    real = in_expert < block_lookup(counts)[:, None]
    spread = (block_row[:, None] + jnp.arange(bm, dtype=jnp.int32)) % n_tok
    row_tok = jnp.where(real, sorted_tok.at[dense_row].get(mode="promise_in_bounds"), spread).reshape(-1)
    row_w = jnp.where(real, sorted_w.at[dense_row].get(mode="promise_in_bounds"), 0.0).reshape(-1, 1)

    xs = h1s.reshape(n_tok, n_slab, LANES).at[row_tok].get(mode="promise_in_bounds")
    ys = _experts(block_e, n_valid, xs.reshape(nb * bm * n_slab, LANES), row_w,
                  w_exp_gate[0], w_exp_up[0], w_exp_down[0], bm=bm)
    yg = ys.reshape(nb * bm, n_slab, LANES).at[dest_tk.reshape(-1)].get(mode="promise_in_bounds")

    out = _final(h1, yg.reshape(n_asg * n_slab, LANES), w_sh_gate[0].astype(BF16), w_sh_up[0].astype(BF16),
                 w_sh_down[0].astype(BF16), row(ln2_g[0]), row(ln2_b[0]), tm=_tile(seq, 256), alpha=alpha)
    return out.reshape(batch, seq, d)
```

```python
import functools

import jax
import jax.numpy as jnp
from jax import lax
from jax.experimental import pallas as pl
from jax.experimental.pallas import tpu as pltpu

CHUNK = 64
ROPE_THETA = 10000.0
LN_EPS = 1e-5
TOP_K = 8
N_GROUPS = 8
TOPK_GROUPS = 4
ROUTED_SCALE = 2.5
LAM_INIT = 0.2
LOG2E = 1.4426950408889634

LANES = 128
BF16_SUBLANES = 16
VMEM_LIMIT_BYTES = 56 * 1024 * 1024

F32 = jnp.float32
BF16 = jnp.bfloat16


def _params(*sem):
    return pltpu.CompilerParams(dimension_semantics=sem, vmem_limit_bytes=VMEM_LIMIT_BYTES)


def _layer_norm(x, g, b):
    mu = jnp.mean(x, axis=-1, keepdims=True)
    xc = x - mu
    var = jnp.mean(xc * xc, axis=-1, keepdims=True)
    return xc * lax.rsqrt(var + LN_EPS) * g + b


def _dot(a, b):
    return jnp.dot(a, b, preferred_element_type=F32)


def _dot_nt(a, b, **kw):
    return lax.dot_general(a, b, (((1,), (1,)), ((), ())), preferred_element_type=F32, **kw)


def _sigmoid(x):
    return 1.0 / (1.0 + jnp.exp(-x))


def _store_slabs(ref, x, stage):
    rows, d = x.shape
    n_slab = d // LANES
    dst = ref if ref.dtype == F32 else stage
    for c in range(n_slab):
        dst[pl.ds(c, rows, stride=n_slab), :] = x[:, c * LANES:(c + 1) * LANES]
    if dst is not ref:
        ref[...] = stage[...].astype(ref.dtype)


def _slab_rows(ref, rows, d):
    n_slab = d // LANES
    return jnp.concatenate([ref[pl.ds(c, rows, stride=n_slab), :] for c in range(n_slab)], axis=1)


def _inproj_kernel(x_ref, g_ref, b_ref, w_ref, cos_ref, sin_ref, u_ref, h_ref, hb_ref,
                   *, q_tiles, qk_scale, half):
    j = pl.program_id(1)

    @pl.when(j == 0)
    def _():
        h = _layer_norm(x_ref[...], g_ref[...], b_ref[...])
        h_ref[...] = h
        hb_ref[...] = h.astype(BF16)

    acc = _dot(hb_ref[...], w_ref[...])
    tn = acc.shape[1]

    @pl.when(j < 2 * q_tiles)
    def _():
        cos = cos_ref[...]
        sin = sin_ref[...]
        lane = lax.broadcasted_iota(jnp.int32, cos.shape, 1)
        first = (lane % (2 * half)) < half
        scale = jnp.where(j < q_tiles, qk_scale, 1.0).astype(F32)
        for c in range(tn // LANES):
            a = acc[:, c * LANES:(c + 1) * LANES]
            partner = jnp.where(first, pltpu.roll(a, LANES - half, axis=1), pltpu.roll(a, half, axis=1))
            u_ref[:, c * LANES:(c + 1) * LANES] = ((a * cos + partner * sin) * scale).astype(BF16)

    @pl.when(j >= 2 * q_tiles)
    def _():
        u_ref[...] = acc.astype(BF16)


def _in_proj(x2d, g, b, w_bf, cos, sin, *, tm, tn, qk_width, dh):
    rows, d = x2d.shape
    cols = w_bf.shape[1]
    pos_blocks = cos.shape[0] // tm
    kern = functools.partial(_inproj_kernel, q_tiles=qk_width // tn, qk_scale=float(dh) ** -0.5 * LOG2E, half=dh // 2)
    return pl.pallas_call(
        kern,
        grid=(rows // tm, cols // tn),
        in_specs=[
            pl.BlockSpec((tm, d), lambda i, j: (i, 0)),
            pl.BlockSpec((1, d), lambda i, j: (0, 0)),
            pl.BlockSpec((1, d), lambda i, j: (0, 0)),
            pl.BlockSpec((d, tn), lambda i, j: (0, j)),
            pl.BlockSpec((tm, LANES), lambda i, j: (i % pos_blocks, 0)),
            pl.BlockSpec((tm, LANES), lambda i, j: (i % pos_blocks, 0)),
        ],
        out_specs=[
            pl.BlockSpec((tm, tn), lambda i, j: (i, j)),
            pl.BlockSpec((tm, d), lambda i, j: (i, 0)),
        ],
        out_shape=[
            jax.ShapeDtypeStruct((rows, cols), BF16),
            jax.ShapeDtypeStruct((rows, d), F32),
        ],
        scratch_shapes=[pltpu.VMEM((tm, d), BF16)],
        compiler_params=_params("arbitrary", "arbitrary"),
        name="in_proj",
    )(x2d, g, b, w_bf, cos, sin)


def _attn_kernel(q_ref, k_ref, v_ref, km_ref, vm_ref, lamv_ref, g_ref, o_ref, vt_ref, vmt_ref, s_ref, *, tq, dh):
    qi = pl.program_id(2)
    seq = k_ref.shape[0]

    @pl.when(qi == 0)
    def _():
        for c in range(seq // tq):
            vt_ref[:, c * tq:(c + 1) * tq] = v_ref[c * tq:(c + 1) * tq, :].astype(F32).T.astype(BF16)
        vmt_ref[...] = vm_ref[...].astype(F32).T.astype(BF16)

    qt = q_ref[...].astype(F32).T
    dim = lax.broadcasted_iota(jnp.int32, qt.shape, 0)
    qq = jnp.concatenate([jnp.where(dim < dh, qt, 0.0), jnp.where(dim >= dh, qt, 0.0)], axis=1).astype(BF16)

    s = _dot(km_ref[...], qq)
    m = jnp.max(s, axis=0, keepdims=True)
    p = jnp.exp2(s - m)
    l = jnp.sum(p, axis=0, keepdims=True)
    acc = _dot(vmt_ref[...], p.astype(BF16))

    def update(carry, s, vt):
        m, l, acc = carry
        m_new = jnp.maximum(m, jnp.max(s, axis=0, keepdims=True))
        a = jnp.exp2(m - m_new)
        p = jnp.exp2(s - m_new)
        l = a * l + jnp.sum(p, axis=0, keepdims=True)
        acc = a * acc + _dot(vt, p.astype(BF16))
        return m_new, l, acc

    def scores(kb):
        return _dot(k_ref[pl.ds(pl.multiple_of(kb * tq, tq), tq), :], qq)

    def values(kb):
        return vt_ref[:, pl.ds(pl.multiple_of(kb * tq, tq), tq)]

    def block_pair(i, carry):
        s_ref[1] = scores(2 * i + 1)
        carry = update(carry, s_ref[0], values(2 * i))
        s_ref[0] = scores(2 * i + 2)
        return update(carry, s_ref[1], values(2 * i + 1))

    def odd_block(carry):
        s_ref[1] = scores(qi)
        carry = update(carry, s_ref[0], values(qi - 1))
        s_ref[0] = s_ref[1]
        return carry

    s_ref[0] = scores(0)
    carry = lax.fori_loop(0, qi // 2, block_pair, (m, l, acc))
    m, l, acc = lax.cond(qi % 2 == 1, odd_block, lambda c: c, carry)
    s = s_ref[0]

    start = pl.multiple_of(qi * tq, tq)
    key = lax.broadcasted_iota(jnp.int32, s.shape, 0)
    qry = lax.broadcasted_iota(jnp.int32, s.shape, 1)
    qry = jnp.where(qry >= tq, qry - tq, qry)
    s = jnp.where((key // CHUNK) <= (qry // CHUNK), s, -jnp.inf)
    m, l, acc = update((m, l, acc), s, vt_ref[:, pl.ds(start, tq)])

    lamv = lamv_ref[...]
    lam = (jnp.exp(jnp.sum(lamv[0:1] * lamv[1:2], axis=1, keepdims=True))
           - jnp.exp(jnp.sum(lamv[2:3] * lamv[3:4], axis=1, keepdims=True)) + LAM_INIT)
    o_all = acc / l
    o = (o_all[:, :tq] - lam * o_all[:, tq:]).T
    ms = jnp.mean(o * o, axis=-1, keepdims=True)
    o = o * lax.rsqrt(ms + LN_EPS) * g_ref[...] * (1.0 - LAM_INIT)
    o_ref[...] = o.astype(BF16)


def _attention(u, u_meta, lamv, subln_g, *, batch, seq, heads, dh, tq):
    vd = 2 * dh
    nq = seq // tq
    kern = functools.partial(_attn_kernel, tq=tq, dh=dh)
    n_meta = u_meta.shape[0]
    return pl.pallas_call(
        kern,
        grid=(batch, heads, nq),
        in_specs=[
            pl.BlockSpec((tq, vd), lambda b, h, i: (b * nq + i, h)),
            pl.BlockSpec((seq, vd), lambda b, h, i: (b, heads + h)),
            pl.BlockSpec((seq, vd), lambda b, h, i: (b, 2 * heads + h)),
            pl.BlockSpec((n_meta, vd), lambda b, h, i: (0, heads + h)),
            pl.BlockSpec((n_meta, vd), lambda b, h, i: (0, 2 * heads + h)),
            pl.BlockSpec((4, dh), lambda b, h, i: (0, 0)),
            pl.BlockSpec((1, vd), lambda b, h, i: (0, 0)),
        ],
        out_specs=pl.BlockSpec((tq, vd), lambda b, h, i: (b * nq + i, h)),
        out_shape=jax.ShapeDtypeStruct((batch * seq, heads * vd), BF16),
        scratch_shapes=[pltpu.VMEM((vd, seq), BF16), pltpu.VMEM((vd, n_meta), BF16),
                        pltpu.VMEM((2, tq, 2 * tq), F32)],
        compiler_params=_params("arbitrary", "arbitrary", "arbitrary"),
        name="attention",
    )(u, u, u, u_meta, u_meta, lamv, subln_g)


def _merge_kernel(o_ref, cx_ref, cc_ref, cb_ref, hx_ref, hc_ref, mx_ref, mc_ref, wconv_ref,
                  ga_ref, gc_ref, wa_ref, wc_ref, out_ref, y_ref, *, tiles_per_seq):
    i = pl.program_id(0)
    j = pl.program_id(1)

    @pl.when(j == 0)
    def _():
        z = cc_ref[...].astype(F32) * cx_ref[...].astype(F32)
        cb = cb_ref[...].astype(F32)
        w = wconv_ref[...]
        w0, w1, w2 = w[0:1], w[1:2], w[2:3]
        y_ref[...] = (cb * (w0 * pltpu.roll(z, 2, axis=0) + w1 * pltpu.roll(z, 1, axis=0) + w2 * z)).astype(BF16)
        hb = BF16_SUBLANES
        first = (i % tiles_per_seq) == 0
        hz_prev = hc_ref[...].astype(F32) * hx_ref[...].astype(F32)
        hz_meta = mc_ref[...].astype(F32) * mx_ref[...].astype(F32)
        hz = jnp.where(first, hz_meta, hz_prev)
        zm1 = hz[hb - 1:hb]
        zm2 = hz[hb - 2:hb - 1]
        zh = z[0:hb]
        row = lax.broadcasted_iota(jnp.int32, zh.shape, 0)
        z1 = jnp.where(row == 0, zm1, pltpu.roll(zh, 1, axis=0))
        z2 = jnp.where(row == 0, zm2, jnp.where(row == 1, zm1, pltpu.roll(zh, 2, axis=0)))
        y_ref[0:hb, :] = (cb[0:hb] * (w0 * z2 + w1 * z1 + w2 * zh)).astype(BF16)

    pa = _dot(o_ref[...], wa_ref[...])
    pc = _dot(y_ref[...], wc_ref[...])
    out = _sigmoid(ga_ref[...].astype(F32)) * pa + _sigmoid(gc_ref[...].astype(F32)) * pc
    out_ref[...] = out.astype(BF16)


def _merge(o_n, u, u_meta, w_conv, wa_bf, wc_bf, *, seq, d, tm, tn):
    rows = o_n.shape[0]
    cd = d // 2
    hb = BF16_SUBLANES
    kern = functools.partial(_merge_kernel, tiles_per_seq=seq // tm)
    halo = lambda c: pl.BlockSpec((hb, cd), lambda i, j: (jnp.maximum(i * (tm // hb) - 1, 0), c))
    return pl.pallas_call(
        kern,
        grid=(rows // tm, d // tn),
        in_specs=[
            pl.BlockSpec((tm, cd), lambda i, j: (i, 0)),
            pl.BlockSpec((tm, cd), lambda i, j: (i, 3)),
            pl.BlockSpec((tm, cd), lambda i, j: (i, 4)),
            pl.BlockSpec((tm, cd), lambda i, j: (i, 5)),
            halo(3),
            halo(4),
            pl.BlockSpec((hb, cd), lambda i, j: (0, 3)),
            pl.BlockSpec((hb, cd), lambda i, j: (0, 4)),
            pl.BlockSpec((3, cd), lambda i, j: (0, 0)),
            pl.BlockSpec((tm, tn), lambda i, j: (i, 3 * d // tn + j)),
            pl.BlockSpec((tm, tn), lambda i, j: (i, 4 * d // tn + j)),
            pl.BlockSpec((cd, tn), lambda i, j: (0, j)),
            pl.BlockSpec((cd, tn), lambda i, j: (0, j)),
        ],
        out_specs=pl.BlockSpec((tm, tn), lambda i, j: (i, j)),
        out_shape=jax.ShapeDtypeStruct((rows, d), BF16),
        scratch_shapes=[pltpu.VMEM((tm, cd), BF16)],
        compiler_params=_params("arbitrary", "arbitrary"),
        name="merge",
    )(o_n, u, u, u, u, u, u_meta, u_meta, w_conv, u, u, wa_bf, wc_bf)


def _out_router_kernel(mg_ref, h_ref, wo_ref, g_ref, b_ref, wr_ref, rb_ref, tri_ref,
                       h1_ref, h1s_ref, eidx_ref, pos_ref, wtk_ref, counts_ref, cnt_ref, *, alpha):
    @pl.when(pl.program_id(0) == 0)
    def _():
        cnt_ref[...] = jnp.zeros_like(cnt_ref)

    m = _dot(mg_ref[...], wo_ref[...])
    h1 = _layer_norm(alpha * h_ref[...] + m, g_ref[...], b_ref[...])
    h1_ref[...] = h1
    _store_slabs(h1s_ref, h1, None)

    logits = _dot_nt(wr_ref[...], h1, precision=lax.Precision.HIGHEST)
    s = _sigmoid(logits)
    sel = s + rb_ref[...]
    n_exp, tm = sel.shape
    per_group = n_exp // N_GROUPS
    neg = -jnp.inf

    grow = lax.broadcasted_iota(jnp.int32, (per_group, tm), 0).astype(F32)
    scores = []
    for g in range(N_GROUPS):
        sg = sel[g * per_group:(g + 1) * per_group]
        m1 = jnp.max(sg, axis=0, keepdims=True)
        first = jnp.min(jnp.where(sg == m1, grow, float(per_group)), axis=0, keepdims=True)
        m2 = jnp.max(jnp.where(grow == first, neg, sg), axis=0, keepdims=True)
        scores.append(m1 + m2)
    gs = jnp.concatenate(scores, axis=0)

    gidx = lax.broadcasted_iota(jnp.int32, gs.shape, 0)
    grank = jnp.zeros(gs.shape, F32)
    for g in range(N_GROUPS):
        o = gs[g:g + 1]
        grank = grank + jnp.where((o > gs) | ((o == gs) & (gidx > g)), 1.0, 0.0)
    gkeep = jnp.where(grank < TOPK_GROUPS, 1.0, 0.0)
    keep = jnp.concatenate([jnp.broadcast_to(gkeep[g:g + 1], (per_group, tm)) for g in range(N_GROUPS)], axis=0)
    selm = jnp.where(keep > 0.5, sel, neg)

    eidx = lax.broadcasted_iota(jnp.int32, selm.shape, 0)
    rank = jnp.zeros(selm.shape, F32)
    for e in range(n_exp):
        o = selm[e:e + 1]
        rank = rank + jnp.where((o > selm) | ((o == selm) & (eidx > e)), 1.0, 0.0)
    chosen = rank < TOP_K
    ssel = jnp.where(chosen, s, 0.0)
    w = ssel / jnp.sum(ssel, axis=0, keepdims=True) * ROUTED_SCALE

    chosen_f = jnp.where(chosen, 1.0, 0.0)
    before = cnt_ref[...]
    pos = _dot(chosen_f.astype(BF16), tri_ref[...]) + before
    cnt_ref[...] = before + jnp.sum(chosen_f, axis=1, keepdims=True)
    counts_ref[...] = cnt_ref[...]

    def pick(v):
        rows = [jnp.sum(jnp.where(rank == float(r), v, 0.0), axis=0, keepdims=True) for r in range(TOP_K)]
        return jnp.concatenate(rows, axis=0)

    eidx_ref[...] = pick(eidx.astype(F32)).astype(jnp.int32)
    pos_ref[...] = pick(pos).astype(jnp.int32)
    wtk_ref[...] = pick(w)


def _out_router(merged, h, wo_bf, g, b, w_router_t, router_bias, *, tm, alpha):
    rows, d = merged.shape
    n_exp = w_router_t.shape[0]
    n_slab = d // LANES
    tri = jnp.triu(jnp.ones((tm, tm), BF16), 1)
    kern = functools.partial(_out_router_kernel, alpha=alpha)
    per_tok = lambda dt: jax.ShapeDtypeStruct((TOP_K, rows), dt)
    return pl.pallas_call(
        kern,
        grid=(rows // tm,),
        in_specs=[
            pl.BlockSpec((tm, d), lambda i: (i, 0)),
            pl.BlockSpec((tm, d), lambda i: (i, 0)),
            pl.BlockSpec((d, d), lambda i: (0, 0), pipeline_mode=pl.Buffered(1)),
            pl.BlockSpec((1, d), lambda i: (0, 0)),
            pl.BlockSpec((1, d), lambda i: (0, 0)),
            pl.BlockSpec((n_exp, d), lambda i: (0, 0), pipeline_mode=pl.Buffered(1)),
            pl.BlockSpec((n_exp, 1), lambda i: (0, 0)),
            pl.BlockSpec((tm, tm), lambda i: (0, 0), pipeline_mode=pl.Buffered(1)),
        ],
        out_specs=[
            pl.BlockSpec((tm, d), lambda i: (i, 0)),
            pl.BlockSpec((tm * n_slab, LANES), lambda i: (i, 0)),
            pl.BlockSpec((TOP_K, tm), lambda i: (0, i)),
            pl.BlockSpec((TOP_K, tm), lambda i: (0, i)),
            pl.BlockSpec((TOP_K, tm), lambda i: (0, i)),
            pl.BlockSpec((n_exp, 1), lambda i: (0, 0)),
        ],
        out_shape=[
            jax.ShapeDtypeStruct((rows, d), F32),
            jax.ShapeDtypeStruct((rows * n_slab, LANES), F32),
            per_tok(jnp.int32),
            per_tok(jnp.int32),
            per_tok(F32),
            jax.ShapeDtypeStruct((n_exp, 1), F32),
        ],
        scratch_shapes=[pltpu.VMEM((n_exp, 1), F32)],
        compiler_params=_params("arbitrary"),
        name="out_router",
    )(merged, h, wo_bf, g, b, w_router_t, router_bias, tri)


def _expert_kernel(be_ref, nv_ref, xs_ref, rw_ref, wg_ref, wu_ref, wd_ref, ys_ref, wg_bf, wu_bf, wd_bf, slot_ref,
                   stage_out):
    s = pl.program_id(0)
    last = pl.num_programs(0) - 2
    cur = be_ref[jnp.maximum(s - 1, 0)]
    nxt = be_ref[jnp.minimum(s, last)]

    @pl.when(s == 0)
    def _():
        slot_ref[0] = 0

    slot = slot_ref[0]
    fill = jnp.where(s == 0, 0, 1 - slot)

    @pl.when((s == 0) | (nxt != cur))
    def _():
        wg_bf[fill] = wg_ref[...].astype(BF16)
        wu_bf[fill] = wu_ref[...].astype(BF16)
        wd_bf[fill] = wd_ref[...].astype(BF16)

    block = s - 1

    @pl.when((block >= 0) & (block < nv_ref[0]))
    def _():
        d, d_exp = wg_ref.shape
        x = _slab_rows(xs_ref, xs_ref.shape[0] * LANES // d, d).astype(BF16)
        g = _dot(x, wg_bf[slot])
        u = _dot(x, wu_bf[slot])
        hid = g * _sigmoid(g) * u
        _store_slabs(ys_ref, _dot(hid.astype(BF16), wd_bf[slot]) * rw_ref[...], stage_out)

    @pl.when(block >= nv_ref[0])
    def _():
        ys_ref[...] = jnp.zeros_like(ys_ref)

    @pl.when((s > 0) & (nxt != cur))
    def _():
        slot_ref[0] = 1 - slot


def _experts(block_e, n_valid, xs, row_w, w_gate, w_up, w_down, *, bm):
    _, d, d_exp = w_gate.shape
    n_slab = d // LANES
    nb = xs.shape[0] // (bm * n_slab)
    ahead = lambda s, be, nv: (be[jnp.minimum(s, nb - 1)], 0, 0)
    grid_spec = pltpu.PrefetchScalarGridSpec(
        num_scalar_prefetch=2,
        grid=(nb + 1,),
        in_specs=[
            pl.BlockSpec((bm * n_slab, LANES), lambda s, be, nv: (jnp.maximum(jnp.minimum(s - 1, nv[0] - 1), 0), 0)),
            pl.BlockSpec((bm, 1), lambda s, be, nv: (jnp.maximum(jnp.minimum(s - 1, nv[0] - 1), 0), 0)),
            pl.BlockSpec((None, d, d_exp), ahead),
            pl.BlockSpec((None, d, d_exp), ahead),
            pl.BlockSpec((None, d_exp, d), ahead),
        ],
        out_specs=pl.BlockSpec((bm * n_slab, LANES), lambda s, be, nv: (jnp.maximum(s - 1, 0), 0)),
        scratch_shapes=[
            pltpu.VMEM((2, d, d_exp), BF16),
            pltpu.VMEM((2, d, d_exp), BF16),
            pltpu.VMEM((2, d_exp, d), BF16),
            pltpu.SMEM((1,), jnp.int32),
            pltpu.VMEM((bm * n_slab, LANES), F32),
        ],
    )
    return pl.pallas_call(
        _expert_kernel,
        grid_spec=grid_spec,
        out_shape=jax.ShapeDtypeStruct(xs.shape, BF16),
        compiler_params=_params("arbitrary"),
        name="experts",
    )(block_e, n_valid, xs, row_w, w_gate, w_up, w_down)


def _final_kernel(*refs, alpha):
    h1_ref = refs[0]
    y_refs = refs[1:1 + TOP_K]
    wg_ref, wu_ref, wd_ref, g_ref, b_ref, out_ref, stage_ref = refs[1 + TOP_K:]
    h1 = h1_ref[...]
    tm, d = h1.shape
    x = h1.astype(BF16)
    gate = _dot(x, wg_ref[...])
    up = _dot(x, wu_ref[...])
    hid = gate * _sigmoid(gate) * up
    shared = _dot(hid.astype(BF16), wd_ref[...])
    acc = y_refs[0][...].astype(F32)
    for r in range(1, TOP_K):
        acc = acc + y_refs[r][...].astype(F32)
    stage_ref[...] = acc
    f = shared + _slab_rows(stage_ref, tm, d)
    out_ref[...] = _layer_norm(alpha * h1 + f, g_ref[...], b_ref[...])


def _final(h1, yg, wg_bf, wu_bf, wd_bf, g, b, *, tm, alpha):
    rows, d = h1.shape
    d_exp = wg_bf.shape[1]
    n_slab = d // LANES
    tiles = rows // tm
    kern = functools.partial(_final_kernel, alpha=alpha)
    y_spec = lambda r: pl.BlockSpec((tm * n_slab, LANES), lambda i: (r * tiles + i, 0))
    return pl.pallas_call(
        kern,
        grid=(tiles,),
        in_specs=[pl.BlockSpec((tm, d), lambda i: (i, 0))] + [y_spec(r) for r in range(TOP_K)] + [
            pl.BlockSpec((d, d_exp), lambda i: (0, 0)),
            pl.BlockSpec((d, d_exp), lambda i: (0, 0)),
            pl.BlockSpec((d_exp, d), lambda i: (0, 0)),
            pl.BlockSpec((1, d), lambda i: (0, 0)),
            pl.BlockSpec((1, d), lambda i: (0, 0)),
        ],
        out_specs=pl.BlockSpec((tm, d), lambda i: (i, 0)),
        out_shape=jax.ShapeDtypeStruct((rows, d), F32),
        scratch_shapes=[pltpu.VMEM((tm * n_slab, LANES), F32)],
        compiler_params=_params("arbitrary"),
        name="final",
    )(h1, *([yg] * TOP_K), wg_bf, wu_bf, wd_bf, g, b)


def _rope_tables(pos, dh):
    half = dh // 2
    inv_freq = ROPE_THETA ** (-jnp.arange(half, dtype=F32) / half)
    ang = pos.astype(F32)[:, None] * inv_freq[None, :]
    cos = jnp.cos(ang)
    sin = jnp.sin(ang)
    reps = LANES // dh
    cos_full = jnp.tile(jnp.concatenate([cos, cos], axis=1), (1, reps))
    sin_full = jnp.tile(jnp.concatenate([-sin, sin], axis=1), (1, reps))
    return cos_full, sin_full


def _tile(n, pref):
    t = min(n, pref)
    assert n % t == 0, (n, pref)
    return t


def kernel(x, meta_tokens, ln0_g, ln0_b, w_in, lambda_q1, lambda_k1, lambda_q2, lambda_k2, subln_g, w_conv, w_proj_attn, w_proj_conv, w_out, ln1_g, ln1_b, w_router, router_bias, w_exp_gate, w_exp_up, w_exp_down, w_sh_gate, w_sh_up, w_sh_down, ln2_g, ln2_b):
    batch, seq, d = x.shape
    n_meta = meta_tokens.shape[0]
    depth = w_in.shape[0]
    dh = lambda_q1.shape[-1]
    n_exp = w_router.shape[-1]
    assert depth == 1 and 2 * dh == LANES and n_meta == BF16_SUBLANES
    assert n_exp // N_GROUPS == 8 and subln_g.shape[-1] == 2 * dh
    cd = d // 2
    heads = cd // (2 * dh)
    n_tok = batch * seq
    alpha = float((2 * depth) ** 0.25)

    row = lambda a: a.reshape(1, -1).astype(F32)

    w_in_bf = w_in[0].astype(BF16)
    cos_m, sin_m = _rope_tables(jnp.arange(n_meta), dh)
    cos_r, sin_r = _rope_tables(jnp.arange(n_meta, n_meta + seq), dh)
    tn_in = _tile(cd, 1024)
    tm_in = _tile(seq, 512)
    inproj = functools.partial(_in_proj, g=row(ln0_g), b=row(ln0_b), w_bf=w_in_bf, tn=tn_in, qk_width=cd, dh=dh)
    u, h = inproj(x.reshape(n_tok, d), cos=cos_r, sin=sin_r, tm=tm_in)
    u_meta, _ = inproj(meta_tokens.astype(F32), cos=cos_m, sin=sin_m, tm=n_meta)

    lamv = jnp.stack([lambda_q1[0], lambda_k1[0], lambda_q2[0], lambda_k2[0]]).astype(F32)
    o_n = _attention(u, u_meta, lamv, row(subln_g[0]), batch=batch, seq=seq, heads=heads, dh=dh,
                     tq=_tile(seq, 512))

    merged = _merge(o_n, u, u_meta, w_conv[0].astype(F32), w_proj_attn[0].astype(BF16),
                    w_proj_conv[0].astype(BF16), seq=seq, d=d, tm=_tile(seq, 512), tn=_tile(d, 1024))

    h1, h1s, eidx, pos_tk, w_tk, counts = _out_router(
        merged, h, w_out[0].astype(BF16), row(ln1_g[0]), row(ln1_b[0]),
        w_router[0].T.astype(F32), router_bias[0].reshape(n_exp, 1).astype(F32),
        tm=_tile(seq, 512), alpha=alpha)

    bm = 256
    n_slab = d // LANES
    n_asg = n_tok * TOP_K
    nb = -(-n_asg // bm) + n_exp
    counts = counts[:, 0].astype(jnp.int32)
    ends = jnp.cumsum(counts)
    starts = ends - counts
    padded = (counts + bm - 1) // bm * bm
    pad_end = jnp.cumsum(padded)
    pad_start = pad_end - padded
    experts = jnp.arange(n_exp, dtype=jnp.int32)
    onehot = eidx[..., None] == experts
    lookup = lambda table: jnp.sum(jnp.where(onehot, table, 0), axis=-1)
    dense_tk = pos_tk + lookup(starts)
    dest_tk = pos_tk + lookup(pad_start)
    tok = jnp.broadcast_to(jnp.arange(n_tok, dtype=jnp.int32)[None], dense_tk.shape)
    _, sorted_tok, sorted_w = lax.sort((dense_tk.reshape(-1), tok.reshape(-1), w_tk.reshape(-1)), num_keys=1)
    block_row = jnp.arange(nb, dtype=jnp.int32) * bm
    block_e = jnp.minimum(jnp.sum(pad_end[None, :] <= block_row[:, None], axis=1), n_exp - 1).astype(jnp.int32)
    n_valid = (pad_end[-1:] // bm).astype(jnp.int32)
    block_hot = block_e[:, None] == experts
    block_lookup = lambda table: jnp.sum(jnp.where(block_hot, table, 0), axis=-1)
    in_expert = (block_row - block_lookup(pad_start))[:, None] + jnp.arange(bm, dtype=jnp.int32)
    dense_row = jnp.minimum(block_lookup(starts)[:, None] + in_expert, n_asg - 1)
    real = in_expert < block_lookup(counts)[:, None]
    spread = (block_row[:, None] + jnp.arange(bm, dtype=jnp.int32)) % n_tok
    row_tok = jnp.where(real, sorted_tok.at[dense_row].get(mode="promise_in_bounds"), spread).reshape(-1)
    row_w = jnp.where(real, sorted_w.at[dense_row].get(mode="promise_in_bounds"), 0.0).reshape(-1, 1)

    xs = h1s.reshape(n_tok, n_slab, LANES).at[row_tok].get(mode="promise_in_bounds")
    ys = _experts(block_e, n_valid, xs.reshape(nb * bm * n_slab, LANES), row_w,
                  w_exp_gate[0], w_exp_up[0], w_exp_down[0], bm=bm)
    yg = ys.reshape(nb * bm, n_slab, LANES).at[dest_tk.reshape(-1)].get(mode="promise_in_bounds")

    out = _final(h1, yg.reshape(n_asg * n_slab, LANES), w_sh_gate[0].astype(BF16), w_sh_up[0].astype(BF16),
                 w_sh_down[0].astype(BF16), row(ln2_g[0]), row(ln2_b[0]), tm=_tile(seq, 256), alpha=alpha)
    return out.reshape(batch, seq, d)
```

```python
import functools

import jax
import jax.numpy as jnp
from jax import lax
from jax.experimental import pallas as pl
from jax.experimental.pallas import tpu as pltpu

CHUNK = 64
ROPE_THETA = 10000.0
LN_EPS = 1e-5
TOP_K = 8
N_GROUPS = 8
TOPK_GROUPS = 4
ROUTED_SCALE = 2.5
LAM_INIT = 0.2
LOG2E = 1.4426950408889634

LANES = 128
BF16_SUBLANES = 16
VMEM_LIMIT_BYTES = 56 * 1024 * 1024

F32 = jnp.float32
BF16 = jnp.bfloat16


def _params(*sem):
    return pltpu.CompilerParams(dimension_semantics=sem, vmem_limit_bytes=VMEM_LIMIT_BYTES)


def _layer_norm(x, g, b):
    mu = jnp.mean(x, axis=-1, keepdims=True)
    xc = x - mu
    var = jnp.mean(xc * xc, axis=-1, keepdims=True)
    return xc * lax.rsqrt(var + LN_EPS) * g + b


def _dot(a, b):
    return jnp.dot(a, b, preferred_element_type=F32)


def _dot_nt(a, b, **kw):
    return lax.dot_general(a, b, (((1,), (1,)), ((), ())), preferred_element_type=F32, **kw)


def _sigmoid(x):
    return 1.0 / (1.0 + jnp.exp(-x))


def _store_slabs(ref, x, stage):
    rows, d = x.shape
    n_slab = d // LANES
    dst = ref if ref.dtype == F32 else stage
    for c in range(n_slab):
        dst[pl.ds(c, rows, stride=n_slab), :] = x[:, c * LANES:(c + 1) * LANES]
    if dst is not ref:
        ref[...] = stage[...].astype(ref.dtype)


def _slab_rows(ref, rows, d):
    n_slab = d // LANES
    return jnp.concatenate([ref[pl.ds(c, rows, stride=n_slab), :] for c in range(n_slab)], axis=1)


def _inproj_kernel(x_ref, g_ref, b_ref, w_ref, cos_ref, sin_ref, u_ref, h_ref, hb_ref,
                   *, q_tiles, qk_scale, half):
    j = pl.program_id(1)

    @pl.when(j == 0)
    def _():
        h = _layer_norm(x_ref[...], g_ref[...], b_ref[...])
        h_ref[...] = h
        hb_ref[...] = h.astype(BF16)

    acc = _dot(hb_ref[...], w_ref[...])
    tn = acc.shape[1]

    @pl.when(j < 2 * q_tiles)
    def _():
        cos = cos_ref[...]
        sin = sin_ref[...]
        lane = lax.broadcasted_iota(jnp.int32, cos.shape, 1)
        first = (lane % (2 * half)) < half
        scale = jnp.where(j < q_tiles, qk_scale, 1.0).astype(F32)
        for c in range(tn // LANES):
            a = acc[:, c * LANES:(c + 1) * LANES]
            partner = jnp.where(first, pltpu.roll(a, LANES - half, axis=1), pltpu.roll(a, half, axis=1))
            u_ref[:, c * LANES:(c + 1) * LANES] = ((a * cos + partner * sin) * scale).astype(BF16)

    @pl.when(j >= 2 * q_tiles)
    def _():
        u_ref[...] = acc.astype(BF16)


def _in_proj(x2d, g, b, w_bf, cos, sin, *, tm, tn, qk_width, dh):
    rows, d = x2d.shape
    cols = w_bf.shape[1]
    pos_blocks = cos.shape[0] // tm
    kern = functools.partial(_inproj_kernel, q_tiles=qk_width // tn, qk_scale=float(dh) ** -0.5 * LOG2E, half=dh // 2)
    return pl.pallas_call(
        kern,
        grid=(rows // tm, cols // tn),
        in_specs=[
            pl.BlockSpec((tm, d), lambda i, j: (i, 0)),
            pl.BlockSpec((1, d), lambda i, j: (0, 0)),
            pl.BlockSpec((1, d), lambda i, j: (0, 0)),
            pl.BlockSpec((d, tn), lambda i, j: (0, j)),
            pl.BlockSpec((tm, LANES), lambda i, j: (i % pos_blocks, 0)),
            pl.BlockSpec((tm, LANES), lambda i, j: (i % pos_blocks, 0)),
        ],
        out_specs=[
            pl.BlockSpec((tm, tn), lambda i, j: (i, j)),
            pl.BlockSpec((tm, d), lambda i, j: (i, 0)),
        ],
        out_shape=[
            jax.ShapeDtypeStruct((rows, cols), BF16),
            jax.ShapeDtypeStruct((rows, d), F32),
        ],
        scratch_shapes=[pltpu.VMEM((tm, d), BF16)],
        compiler_params=_params("arbitrary", "arbitrary"),
        name="in_proj",
    )(x2d, g, b, w_bf, cos, sin)


def _attn_kernel(q_ref, k_ref, v_ref, km_ref, vm_ref, lamv_ref, g_ref, o_ref, vt_ref, vmt_ref, s_ref, *, tq, dh):
    qi = pl.program_id(2)
    seq = k_ref.shape[0]

    @pl.when(qi == 0)
    def _():
        for c in range(seq // tq):
            vt_ref[:, c * tq:(c + 1) * tq] = v_ref[c * tq:(c + 1) * tq, :].astype(F32).T.astype(BF16)
        vmt_ref[...] = vm_ref[...].astype(F32).T.astype(BF16)

    qt = q_ref[...].astype(F32).T
    dim = lax.broadcasted_iota(jnp.int32, qt.shape, 0)
    qq = jnp.concatenate([jnp.where(dim < dh, qt, 0.0), jnp.where(dim >= dh, qt, 0.0)], axis=1).astype(BF16)

    s = _dot(km_ref[...], qq)
    m = jnp.max(s, axis=0, keepdims=True)
    p = jnp.exp2(s - m)
    l = jnp.sum(p, axis=0, keepdims=True)
    acc = _dot(vmt_ref[...], p.astype(BF16))

    def update(carry, s, vt):
        m, l, acc = carry
        m_new = jnp.maximum(m, jnp.max(s, axis=0, keepdims=True))
        a = jnp.exp2(m - m_new)
        p = jnp.exp2(s - m_new)
        l = a * l + jnp.sum(p, axis=0, keepdims=True)
        acc = a * acc + _dot(vt, p.astype(BF16))
        return m_new, l, acc

    def scores(kb):
        return _dot(k_ref[pl.ds(pl.multiple_of(kb * tq, tq), tq), :], qq)

    def values(kb):
        return vt_ref[:, pl.ds(pl.multiple_of(kb * tq, tq), tq)]

    def block_pair(i, carry):
        s_ref[1] = scores(2 * i + 1)
        carry = update(carry, s_ref[0], values(2 * i))
        s_ref[0] = scores(2 * i + 2)
        return update(carry, s_ref[1], values(2 * i + 1))

    def odd_block(carry):
        s_ref[1] = scores(qi)
        carry = update(carry, s_ref[0], values(qi - 1))
        s_ref[0] = s_ref[1]
        return carry

    s_ref[0] = scores(0)
    carry = lax.fori_loop(0, qi // 2, block_pair, (m, l, acc))
    m, l, acc = lax.cond(qi % 2 == 1, odd_block, lambda c: c, carry)
    s = s_ref[0]

    start = pl.multiple_of(qi * tq, tq)
    key = lax.broadcasted_iota(jnp.int32, s.shape, 0)
    qry = lax.broadcasted_iota(jnp.int32, s.shape, 1)
    qry = jnp.where(qry >= tq, qry - tq, qry)
    s = jnp.where((key // CHUNK) <= (qry // CHUNK), s, -jnp.inf)
    m, l, acc = update((m, l, acc), s, vt_ref[:, pl.ds(start, tq)])

    lamv = lamv_ref[...]
    lam = (jnp.exp(jnp.sum(lamv[0:1] * lamv[1:2], axis=1, keepdims=True))
           - jnp.exp(jnp.sum(lamv[2:3] * lamv[3:4], axis=1, keepdims=True)) + LAM_INIT)
    o_all = acc / l
    o = (o_all[:, :tq] - lam * o_all[:, tq:]).T
    ms = jnp.mean(o * o, axis=-1, keepdims=True)
    o = o * lax.rsqrt(ms + LN_EPS) * g_ref[...] * (1.0 - LAM_INIT)
    o_ref[...] = o.astype(BF16)


def _attention(u, u_meta, lamv, subln_g, *, batch, seq, heads, dh, tq):
    vd = 2 * dh
    nq = seq // tq
    kern = functools.partial(_attn_kernel, tq=tq, dh=dh)
    n_meta = u_meta.shape[0]
    return pl.pallas_call(
        kern,
        grid=(batch, heads, nq),
        in_specs=[
            pl.BlockSpec((tq, vd), lambda b, h, i: (b * nq + i, h)),
            pl.BlockSpec((seq, vd), lambda b, h, i: (b, heads + h)),
            pl.BlockSpec((seq, vd), lambda b, h, i: (b, 2 * heads + h)),
            pl.BlockSpec((n_meta, vd), lambda b, h, i: (0, heads + h)),
            pl.BlockSpec((n_meta, vd), lambda b, h, i: (0, 2 * heads + h)),
            pl.BlockSpec((4, dh), lambda b, h, i: (0, 0)),
            pl.BlockSpec((1, vd), lambda b, h, i: (0, 0)),
        ],
        out_specs=pl.BlockSpec((tq, vd), lambda b, h, i: (b * nq + i, h)),
        out_shape=jax.ShapeDtypeStruct((batch * seq, heads * vd), BF16),
        scratch_shapes=[pltpu.VMEM((vd, seq), BF16), pltpu.VMEM((vd, n_meta), BF16),
                        pltpu.VMEM((2, tq, 2 * tq), F32)],
        compiler_params=_params("arbitrary", "arbitrary", "arbitrary"),
        name="attention",
    )(u, u, u, u_meta, u_meta, lamv, subln_g)


def _merge_kernel(o_ref, cx_ref, cc_ref, cb_ref, hx_ref, hc_ref, mx_ref, mc_ref, wconv_ref,
                  ga_ref, gc_ref, wa_ref, wc_ref, out_ref, y_ref, *, tiles_per_seq):
    i = pl.program_id(0)
    j = pl.program_id(1)

    @pl.when(j == 0)
    def _():
        z = cc_ref[...].astype(F32) * cx_ref[...].astype(F32)
        cb = cb_ref[...].astype(F32)
        w = wconv_ref[...]
        w0, w1, w2 = w[0:1], w[1:2], w[2:3]
        y_ref[...] = (cb * (w0 * pltpu.roll(z, 2, axis=0) + w1 * pltpu.roll(z, 1, axis=0) + w2 * z)).astype(BF16)
        hb = BF16_SUBLANES
        first = (i % tiles_per_seq) == 0
        hz_prev = hc_ref[...].astype(F32) * hx_ref[...].astype(F32)
        hz_meta = mc_ref[...].astype(F32) * mx_ref[...].astype(F32)
        hz = jnp.where(first, hz_meta, hz_prev)
        zm1 = hz[hb - 1:hb]
        zm2 = hz[hb - 2:hb - 1]
        zh = z[0:hb]
        row = lax.broadcasted_iota(jnp.int32, zh.shape, 0)
        z1 = jnp.where(row == 0, zm1, pltpu.roll(zh, 1, axis=0))
        z2 = jnp.where(row == 0, zm2, jnp.where(row == 1, zm1, pltpu.roll(zh, 2, axis=0)))
        y_ref[0:hb, :] = (cb[0:hb] * (w0 * z2 + w1 * z1 + w2 * zh)).astype(BF16)

    pa = _dot(o_ref[...], wa_ref[...])
    pc = _dot(y_ref[...], wc_ref[...])
    out = _sigmoid(ga_ref[...].astype(F32)) * pa + _sigmoid(gc_ref[...].astype(F32)) * pc
    out_ref[...] = out.astype(BF16)


def _merge(o_n, u, u_meta, w_conv, wa_bf, wc_bf, *, seq, d, tm, tn):
    rows = o_n.shape[0]
    cd = d // 2
    hb = BF16_SUBLANES
    kern = functools.partial(_merge_kernel, tiles_per_seq=seq // tm)
    halo = lambda c: pl.BlockSpec((hb, cd), lambda i, j: (jnp.maximum(i * (tm // hb) - 1, 0), c))
    once = dict(pipeline_mode=pl.Buffered(1)) if tn == d else {}
    return pl.pallas_call(
        kern,
        grid=(rows // tm, d // tn),
        in_specs=[
            pl.BlockSpec((tm, cd), lambda i, j: (i, 0)),
            pl.BlockSpec((tm, cd), lambda i, j: (i, 3)),
            pl.BlockSpec((tm, cd), lambda i, j: (i, 4)),
            pl.BlockSpec((tm, cd), lambda i, j: (i, 5)),
            halo(3),
            halo(4),
            pl.BlockSpec((hb, cd), lambda i, j: (0, 3)),
            pl.BlockSpec((hb, cd), lambda i, j: (0, 4)),
            pl.BlockSpec((3, cd), lambda i, j: (0, 0)),
            pl.BlockSpec((tm, tn), lambda i, j: (i, 3 * d // tn + j)),
            pl.BlockSpec((tm, tn), lambda i, j: (i, 4 * d // tn + j)),
            pl.BlockSpec((cd, tn), lambda i, j: (0, j), **once),
            pl.BlockSpec((cd, tn), lambda i, j: (0, j), **once),
        ],
        out_specs=pl.BlockSpec((tm, tn), lambda i, j: (i, j)),
        out_shape=jax.ShapeDtypeStruct((rows, d), BF16),
        scratch_shapes=[pltpu.VMEM((tm, cd), BF16)],
        compiler_params=_params("arbitrary", "arbitrary"),
        name="merge",
    )(o_n, u, u, u, u, u, u_meta, u_meta, w_conv, u, u, wa_bf, wc_bf)


def _out_router_kernel(mg_ref, h_ref, wo_ref, g_ref, b_ref, wr_ref, rb_ref, tri_ref,
                       h1_ref, h1s_ref, eidx_ref, pos_ref, wtk_ref, counts_ref, cnt_ref, *, alpha):
    @pl.when(pl.program_id(0) == 0)
    def _():
        cnt_ref[...] = jnp.zeros_like(cnt_ref)

    m = _dot(mg_ref[...], wo_ref[...])
    h1 = _layer_norm(alpha * h_ref[...] + m, g_ref[...], b_ref[...])
    h1_ref[...] = h1
    _store_slabs(h1s_ref, h1, None)

    logits = _dot_nt(wr_ref[...], h1, precision=lax.Precision.HIGHEST)
    s = _sigmoid(logits)
    sel = s + rb_ref[...]
    n_exp, tm = sel.shape
    per_group = n_exp // N_GROUPS
    neg = -jnp.inf

    grow = lax.broadcasted_iota(jnp.int32, (per_group, tm), 0).astype(F32)
    scores = []
    for g in range(N_GROUPS):
        sg = sel[g * per_group:(g + 1) * per_group]
        m1 = jnp.max(sg, axis=0, keepdims=True)
        first = jnp.min(jnp.where(sg == m1, grow, float(per_group)), axis=0, keepdims=True)
        m2 = jnp.max(jnp.where(grow == first, neg, sg), axis=0, keepdims=True)
        scores.append(m1 + m2)
    gs = jnp.concatenate(scores, axis=0)

    gidx = lax.broadcasted_iota(jnp.int32, gs.shape, 0)
    grank = jnp.zeros(gs.shape, F32)
    for g in range(N_GROUPS):
        o = gs[g:g + 1]
        grank = grank + jnp.where((o > gs) | ((o == gs) & (gidx > g)), 1.0, 0.0)
    gkeep = jnp.where(grank < TOPK_GROUPS, 1.0, 0.0)
    keep = jnp.concatenate([jnp.broadcast_to(gkeep[g:g + 1], (per_group, tm)) for g in range(N_GROUPS)], axis=0)
    selm = jnp.where(keep > 0.5, sel, neg)

    eidx = lax.broadcasted_iota(jnp.int32, selm.shape, 0)
    rank = jnp.zeros(selm.shape, F32)
    for e in range(n_exp):
        o = selm[e:e + 1]
        rank = rank + jnp.where((o > selm) | ((o == selm) & (eidx > e)), 1.0, 0.0)
    chosen = rank < TOP_K
    ssel = jnp.where(chosen, s, 0.0)
    w = ssel / jnp.sum(ssel, axis=0, keepdims=True) * ROUTED_SCALE

    chosen_f = jnp.where(chosen, 1.0, 0.0)
    before = cnt_ref[...]
    pos = _dot(chosen_f.astype(BF16), tri_ref[...]) + before
    cnt_ref[...] = before + jnp.sum(chosen_f, axis=1, keepdims=True)
    counts_ref[...] = cnt_ref[...]

    def pick(v):
        rows = [jnp.sum(jnp.where(rank == float(r), v, 0.0), axis=0, keepdims=True) for r in range(TOP_K)]
        return jnp.concatenate(rows, axis=0)

    eidx_ref[...] = pick(eidx.astype(F32)).astype(jnp.int32)
    pos_ref[...] = pick(pos).astype(jnp.int32)
    wtk_ref[...] = pick(w)


def _out_router(merged, h, wo_bf, g, b, w_router_t, router_bias, *, tm, alpha):
    rows, d = merged.shape
    n_exp = w_router_t.shape[0]
    n_slab = d // LANES
    tri = jnp.triu(jnp.ones((tm, tm), BF16), 1)
    kern = functools.partial(_out_router_kernel, alpha=alpha)
    per_tok = lambda dt: jax.ShapeDtypeStruct((TOP_K, rows), dt)
    return pl.pallas_call(
        kern,
        grid=(rows // tm,),
        in_specs=[
            pl.BlockSpec((tm, d), lambda i: (i, 0)),
            pl.BlockSpec((tm, d), lambda i: (i, 0)),
            pl.BlockSpec((d, d), lambda i: (0, 0), pipeline_mode=pl.Buffered(1)),
            pl.BlockSpec((1, d), lambda i: (0, 0)),
            pl.BlockSpec((1, d), lambda i: (0, 0)),
            pl.BlockSpec((n_exp, d), lambda i: (0, 0), pipeline_mode=pl.Buffered(1)),
            pl.BlockSpec((n_exp, 1), lambda i: (0, 0)),
            pl.BlockSpec((tm, tm), lambda i: (0, 0), pipeline_mode=pl.Buffered(1)),
        ],
        out_specs=[
            pl.BlockSpec((tm, d), lambda i: (i, 0)),
            pl.BlockSpec((tm * n_slab, LANES), lambda i: (i, 0)),
            pl.BlockSpec((TOP_K, tm), lambda i: (0, i)),
            pl.BlockSpec((TOP_K, tm), lambda i: (0, i)),
            pl.BlockSpec((TOP_K, tm), lambda i: (0, i)),
            pl.BlockSpec((n_exp, 1), lambda i: (0, 0)),
        ],
        out_shape=[
            jax.ShapeDtypeStruct((rows, d), F32),
            jax.ShapeDtypeStruct((rows * n_slab, LANES), F32),
            per_tok(jnp.int32),
            per_tok(jnp.int32),
            per_tok(F32),
            jax.ShapeDtypeStruct((n_exp, 1), F32),
        ],
        scratch_shapes=[pltpu.VMEM((n_exp, 1), F32)],
        compiler_params=_params("arbitrary"),
        name="out_router",
    )(merged, h, wo_bf, g, b, w_router_t, router_bias, tri)


def _expert_kernel(be_ref, nv_ref, xs_ref, rw_ref, wg_ref, wu_ref, wd_ref, ys_ref, wg_bf, wu_bf, wd_bf, slot_ref,
                   stage_out):
    s = pl.program_id(0)
    last = pl.num_programs(0) - 2
    cur = be_ref[jnp.maximum(s - 1, 0)]
    nxt = be_ref[jnp.minimum(s, last)]

    @pl.when(s == 0)
    def _():
        slot_ref[0] = 0

    slot = slot_ref[0]
    fill = jnp.where(s == 0, 0, 1 - slot)

    @pl.when((s == 0) | (nxt != cur))
    def _():
        wg_bf[fill] = wg_ref[...].astype(BF16)
        wu_bf[fill] = wu_ref[...].astype(BF16)
        wd_bf[fill] = wd_ref[...].astype(BF16)

    block = s - 1

    @pl.when((block >= 0) & (block < nv_ref[0]))
    def _():
        d, d_exp = wg_ref.shape
        x = _slab_rows(xs_ref, xs_ref.shape[0] * LANES // d, d).astype(BF16)
        g = _dot(x, wg_bf[slot])
        u = _dot(x, wu_bf[slot])
        hid = g * _sigmoid(g) * u
        _store_slabs(ys_ref, _dot(hid.astype(BF16), wd_bf[slot]) * rw_ref[...], stage_out)

    @pl.when(block >= nv_ref[0])
    def _():
        ys_ref[...] = jnp.zeros_like(ys_ref)

    @pl.when((s > 0) & (nxt != cur))
    def _():
        slot_ref[0] = 1 - slot


def _experts(block_e, n_valid, xs, row_w, w_gate, w_up, w_down, *, bm):
    _, d, d_exp = w_gate.shape
    n_slab = d // LANES
    nb = xs.shape[0] // (bm * n_slab)
    ahead = lambda s, be, nv: (be[jnp.minimum(s, nb - 1)], 0, 0)
    grid_spec = pltpu.PrefetchScalarGridSpec(
        num_scalar_prefetch=2,
        grid=(nb + 1,),
        in_specs=[
            pl.BlockSpec((bm * n_slab, LANES), lambda s, be, nv: (jnp.maximum(jnp.minimum(s - 1, nv[0] - 1), 0), 0)),
            pl.BlockSpec((bm, 1), lambda s, be, nv: (jnp.maximum(jnp.minimum(s - 1, nv[0] - 1), 0), 0)),
            pl.BlockSpec((None, d, d_exp), ahead),
            pl.BlockSpec((None, d, d_exp), ahead),
            pl.BlockSpec((None, d_exp, d), ahead),
        ],
        out_specs=pl.BlockSpec((bm * n_slab, LANES), lambda s, be, nv: (jnp.maximum(s - 1, 0), 0)),
        scratch_shapes=[
            pltpu.VMEM((2, d, d_exp), BF16),
            pltpu.VMEM((2, d, d_exp), BF16),
            pltpu.VMEM((2, d_exp, d), BF16),
            pltpu.SMEM((1,), jnp.int32),
            pltpu.VMEM((bm * n_slab, LANES), F32),
        ],
    )
    return pl.pallas_call(
        _expert_kernel,
        grid_spec=grid_spec,
        out_shape=jax.ShapeDtypeStruct(xs.shape, BF16),
        compiler_params=_params("arbitrary"),
        name="experts",
    )(block_e, n_valid, xs, row_w, w_gate, w_up, w_down)


def _final_kernel(*refs, alpha):
    h1_ref = refs[0]
    y_refs = refs[1:1 + TOP_K]
    wg_ref, wu_ref, wd_ref, g_ref, b_ref, out_ref, stage_ref = refs[1 + TOP_K:]
    h1 = h1_ref[...]
    tm, d = h1.shape
    x = h1.astype(BF16)
    gate = _dot(x, wg_ref[...])
    up = _dot(x, wu_ref[...])
    hid = gate * _sigmoid(gate) * up
    shared = _dot(hid.astype(BF16), wd_ref[...])
    acc = y_refs[0][...].astype(F32)
    for r in range(1, TOP_K):
        acc = acc + y_refs[r][...].astype(F32)
    stage_ref[...] = acc
    f = shared + _slab_rows(stage_ref, tm, d)
    out_ref[...] = _layer_norm(alpha * h1 + f, g_ref[...], b_ref[...])


def _final(h1, yg, wg_bf, wu_bf, wd_bf, g, b, *, tm, alpha):
    rows, d = h1.shape
    d_exp = wg_bf.shape[1]
    n_slab = d // LANES
    tiles = rows // tm
    kern = functools.partial(_final_kernel, alpha=alpha)
    y_spec = lambda r: pl.BlockSpec((tm * n_slab, LANES), lambda i: (r * tiles + i, 0))
    return pl.pallas_call(
        kern,
        grid=(tiles,),
        in_specs=[pl.BlockSpec((tm, d), lambda i: (i, 0))] + [y_spec(r) for r in range(TOP_K)] + [
            pl.BlockSpec((d, d_exp), lambda i: (0, 0)),
            pl.BlockSpec((d, d_exp), lambda i: (0, 0)),
            pl.BlockSpec((d_exp, d), lambda i: (0, 0)),
            pl.BlockSpec((1, d), lambda i: (0, 0)),
            pl.BlockSpec((1, d), lambda i: (0, 0)),
        ],
        out_specs=pl.BlockSpec((tm, d), lambda i: (i, 0)),
        out_shape=jax.ShapeDtypeStruct((rows, d), F32),
        scratch_shapes=[pltpu.VMEM((tm * n_slab, LANES), F32)],
        compiler_params=_params("arbitrary"),
        name="final",
    )(h1, *([yg] * TOP_K), wg_bf, wu_bf, wd_bf, g, b)


def _rope_tables(pos, dh):
    half = dh // 2
    inv_freq = ROPE_THETA ** (-jnp.arange(half, dtype=F32) / half)
    ang = pos.astype(F32)[:, None] * inv_freq[None, :]
    cos = jnp.cos(ang)
    sin = jnp.sin(ang)
    reps = LANES // dh
    cos_full = jnp.tile(jnp.concatenate([cos, cos], axis=1), (1, reps))
    sin_full = jnp.tile(jnp.concatenate([-sin, sin], axis=1), (1, reps))
    return cos_full, sin_full


def _tile(n, pref):
    t = min(n, pref)
    assert n % t == 0, (n, pref)
    return t


def kernel(x, meta_tokens, ln0_g, ln0_b, w_in, lambda_q1, lambda_k1, lambda_q2, lambda_k2, subln_g, w_conv, w_proj_attn, w_proj_conv, w_out, ln1_g, ln1_b, w_router, router_bias, w_exp_gate, w_exp_up, w_exp_down, w_sh_gate, w_sh_up, w_sh_down, ln2_g, ln2_b):
    batch, seq, d = x.shape
    n_meta = meta_tokens.shape[0]
    depth = w_in.shape[0]
    dh = lambda_q1.shape[-1]
    n_exp = w_router.shape[-1]
    assert depth == 1 and 2 * dh == LANES and n_meta == BF16_SUBLANES
    assert n_exp // N_GROUPS == 8 and subln_g.shape[-1] == 2 * dh
    cd = d // 2
    heads = cd // (2 * dh)
    n_tok = batch * seq
    alpha = float((2 * depth) ** 0.25)

    row = lambda a: a.reshape(1, -1).astype(F32)

    w_in_bf = w_in[0].astype(BF16)
    cos_m, sin_m = _rope_tables(jnp.arange(n_meta), dh)
    cos_r, sin_r = _rope_tables(jnp.arange(n_meta, n_meta + seq), dh)
    tn_in = _tile(cd, 1024)
    tm_in = _tile(seq, 512)
    inproj = functools.partial(_in_proj, g=row(ln0_g), b=row(ln0_b), w_bf=w_in_bf, tn=tn_in, qk_width=cd, dh=dh)
    u, h = inproj(x.reshape(n_tok, d), cos=cos_r, sin=sin_r, tm=tm_in)
    u_meta, _ = inproj(meta_tokens.astype(F32), cos=cos_m, sin=sin_m, tm=n_meta)

    lamv = jnp.stack([lambda_q1[0], lambda_k1[0], lambda_q2[0], lambda_k2[0]]).astype(F32)
    o_n = _attention(u, u_meta, lamv, row(subln_g[0]), batch=batch, seq=seq, heads=heads, dh=dh,
                     tq=_tile(seq, 512))

    merged = _merge(o_n, u, u_meta, w_conv[0].astype(F32), w_proj_attn[0].astype(BF16),
                    w_proj_conv[0].astype(BF16), seq=seq, d=d, tm=_tile(seq, 512), tn=_tile(d, 2048))

    h1, h1s, eidx, pos_tk, w_tk, counts = _out_router(
        merged, h, w_out[0].astype(BF16), row(ln1_g[0]), row(ln1_b[0]),
        w_router[0].T.astype(F32), router_bias[0].reshape(n_exp, 1).astype(F32),
        tm=_tile(seq, 512), alpha=alpha)

    bm = 256
    n_slab = d // LANES
    n_asg = n_tok * TOP_K
    nb = -(-n_asg // bm) + n_exp
    counts = counts[:, 0].astype(jnp.int32)
    ends = jnp.cumsum(counts)
    starts = ends - counts
    padded = (counts + bm - 1) // bm * bm
    pad_end = jnp.cumsum(padded)
    pad_start = pad_end - padded
    experts = jnp.arange(n_exp, dtype=jnp.int32)
    onehot = eidx[..., None] == experts
    lookup = lambda table: jnp.sum(jnp.where(onehot, table, 0), axis=-1)
    dense_tk = pos_tk + lookup(starts)
    dest_tk = pos_tk + lookup(pad_start)
    tok = jnp.broadcast_to(jnp.arange(n_tok, dtype=jnp.int32)[None], dense_tk.shape)
    _, sorted_tok, sorted_w = lax.sort((dense_tk.reshape(-1), tok.reshape(-1), w_tk.reshape(-1)), num_keys=1)
    block_row = jnp.arange(nb, dtype=jnp.int32) * bm
    block_e = jnp.minimum(jnp.sum(pad_end[None, :] <= block_row[:, None], axis=1), n_exp - 1).astype(jnp.int32)
    n_valid = (pad_end[-1:] // bm).astype(jnp.int32)
    block_hot = block_e[:, None] == experts
    block_lookup = lambda table: jnp.sum(jnp.where(block_hot, table, 0), axis=-1)
    in_expert = (block_row - block_lookup(pad_start))[:, None] + jnp.arange(bm, dtype=jnp.int32)
    dense_row = jnp.minimum(block_lookup(starts)[:, None] + in_expert, n_asg - 1)
    real = in_expert < block_lookup(counts)[:, None]
    spread = (block_row[:, None] + jnp.arange(bm, dtype=jnp.int32)) % n_tok
    row_tok = jnp.where(real, sorted_tok.at[dense_row].get(mode="promise_in_bounds"), spread).reshape(-1)
    row_w = jnp.where(real, sorted_w.at[dense_row].get(mode="promise_in_bounds"), 0.0).reshape(-1, 1)

    xs = h1s.reshape(n_tok, n_slab, LANES).at[row_tok].get(mode="promise_in_bounds")
    ys = _experts(block_e, n_valid, xs.reshape(nb * bm * n_slab, LANES), row_w,
                  w_exp_gate[0], w_exp_up[0], w_exp_down[0], bm=bm)
    yg = ys.reshape(nb * bm, n_slab, LANES).at[dest_tk.reshape(-1)].get(mode="promise_in_bounds")

    out = _final(h1, yg.reshape(n_asg * n_slab, LANES), w_sh_gate[0].astype(BF16), w_sh_up[0].astype(BF16),
                 w_sh_down[0].astype(BF16), row(ln2_g[0]), row(ln2_b[0]), tm=_tile(seq, 256), alpha=alpha)
    return out.reshape(batch, seq, d)
```
